```python
import math
import jax
import jax.numpy as jnp
from jax import lax
import numpy as np

D_MODEL = 4096
BATCH = 32
SEQ = 256
DEPTH = 2
DEC_BATCH = 8
DEC_SEQ = 2048
PAST_LEN = 256

GRID_W = 64
M_HEADS = 64
M_HEADDIM = 64
M_INNER = M_HEADS * M_HEADDIM
M_GROUPS = 8
M_HPG = M_HEADS // M_GROUPS
M_STATE = 128
M_CONV_DIM = M_INNER + 2 * M_GROUPS * M_STATE
CONV_W = 5
HG_HEADS = 32
HG_DK = 128
HG_DV = 128
HG_WIDTH = HG_HEADS * HG_DK
RET_HEADS = 16
RET_DK = 256
RET_DV = 256
RET_WIDTH = RET_HEADS * RET_DK
CHUNK = 64
HG_CHUNK = 32
N_KEYS = 128
N_EXPERTS = N_KEYS * N_KEYS
PK_HEADS = 8
PK_DIM = 256
PK_HALF = PK_DIM // 2
PK_TOPK = 16
PEER_BLOCK = 64
N_BRANCH = 3
ALPHA = (2 * DEPTH) ** 0.25
BETA = (8 * DEPTH) ** -0.25
ROPE_BASE = 10000.0
EPS = 1e-6
IN_SIZES = (M_INNER, M_CONV_DIM, 2 * M_HEADS, HG_WIDTH, 2 * HG_WIDTH, HG_WIDTH, HG_WIDTH, RET_WIDTH, RET_WIDTH, RET_WIDTH, RET_WIDTH, N_BRANCH * D_MODEL)
IN_COLS = sum(IN_SIZES)

kernel_name = 'hybrid_ssd_hgrn2_retention_peer_diffusion_step'


def _split(x, sizes):
    offsets = np.cumsum(sizes)[:-1].tolist()
    return jnp.split(x, offsets, axis=-1)


def _layer_norm(x, g, b):
    xf = x.astype(jnp.float32)
    mu = jnp.mean(xf, -1, keepdims=True)
    var = jnp.mean(jnp.square(xf - mu), -1, keepdims=True)
    return ((xf - mu) * lax.rsqrt(var + EPS) * g + b).astype(x.dtype)


def _group_rms(x, w, groups):
    b, t, width = x.shape
    xg = x.astype(jnp.float32).reshape(b, t, groups, width // groups)
    xg = xg * lax.rsqrt(jnp.mean(jnp.square(xg), -1, keepdims=True) + EPS)
    return xg.reshape(b, t, width) * w


def _dwconv_centered(x, w, bias):
    pad = CONV_W // 2
    y = lax.conv_general_dilated(x, w[:, None, :].astype(x.dtype), window_strides=(1,), padding=[(pad, pad)], dimension_numbers=('NWC', 'WIO', 'NWC'), feature_group_count=x.shape[-1])
    return y + bias


def _rope_2d(x, rows, cols):
    half = RET_DK // 2
    quarter = half // 2
    freqs = ROPE_BASE ** (-jnp.arange(quarter, dtype=jnp.float32) / quarter)

    def rot(xp, p):
        ang = p.astype(jnp.float32)[:, None] * freqs
        cos = jnp.cos(ang)[None, :, None, :]
        sin = jnp.sin(ang)[None, :, None, :]
        x1 = xp[..., :quarter].astype(jnp.float32)
        x2 = xp[..., quarter:].astype(jnp.float32)
        return jnp.concatenate([x1 * cos - x2 * sin, x1 * sin + x2 * cos], -1)

    return jnp.concatenate([rot(x[..., :half], rows), rot(x[..., half:], cols)], -1).astype(x.dtype)


def _ssd_chunk_scan(q, k, v, log_a, s0):
    b, t, g, n = q.shape
    h, p = v.shape[-2:]
    nc = t // CHUNK
    qc = q.reshape(b, nc, CHUNK, g, n)
    kc = k.reshape(b, nc, CHUNK, g, n)
    vc = v.reshape(b, nc, CHUNK, g, h, p)
    cum = jnp.cumsum(log_a.astype(jnp.float32).reshape(b, nc, CHUNK, g, h), axis=2)
    causal = jnp.tril(jnp.ones((CHUNK, CHUNK), bool))
    seg = cum[:, :, :, None] - cum[:, :, None, :]
    decay = jnp.exp(jnp.where(causal[:, :, None, None], seg, -jnp.inf))
    scores = jnp.einsum('bclgn,bcsgn->bclsg', qc, kc)
    y_intra = jnp.einsum('bclsgh,bcsghp->bclghp', scores[..., None] * decay, vc)

    def step(s, inp):
        q_i, k_i, v_i, cum_i = inp
        y_i = jnp.einsum('blgn,blgh,bghnp->blghp', q_i, jnp.exp(cum_i), s)
        tail = jnp.exp(cum_i[:, -1:] - cum_i)
        s = s * jnp.exp(cum_i[:, -1])[..., None, None] + jnp.einsum('blgn,blgh,blghp->bghnp', k_i, tail, v_i)
        return s, y_i

    xs = tuple(jnp.moveaxis(a, 1, 0) for a in (qc, kc, vc, cum))
    s_fin, y_inter = lax.scan(step, s0.astype(jnp.float32), xs)
    y = y_intra + jnp.moveaxis(y_inter, 0, 1)
    return y.reshape(b, t, g, h, p).astype(v.dtype), s_fin


def _gla_chunk_scan(q, k, v, log_f, s0):
    b, t, h, dk = q.shape
    dv = v.shape[-1]
    nc = t // HG_CHUNK
    qc = q.astype(jnp.float32).reshape(b, nc, HG_CHUNK, h, dk)
    kc = k.astype(jnp.float32).reshape(b, nc, HG_CHUNK, h, dk)
    vc = v.reshape(b, nc, HG_CHUNK, h, dv)
    cum = jnp.cumsum(log_f.astype(jnp.float32).reshape(b, nc, HG_CHUNK, h, dk), axis=2)
    ref = cum[:, :, HG_CHUNK // 2][:, :, None]
    causal = jnp.tril(jnp.ones((HG_CHUNK, HG_CHUNK), bool))
    scores = jnp.einsum('bclhk,bcshk->bchls', qc * jnp.exp(cum - ref), kc * jnp.exp(ref - cum))
    scores = jnp.where(causal, scores, 0.0)
    y_intra = jnp.einsum('bchls,bcshv->bclhv', scores, vc)

    def step(s, inp):
        q_i, k_i, v_i, cum_i = inp
        y_i = jnp.einsum('blhk,bhkv->blhv', q_i * jnp.exp(cum_i), s)
        k_dec = k_i * jnp.exp(cum_i[:, -1:] - cum_i)
        s = s * jnp.exp(cum_i[:, -1])[..., None] + jnp.einsum('blhk,blhv->bhkv', k_dec, v_i)
        return s, y_i

    xs = tuple(jnp.moveaxis(a, 1, 0) for a in (qc, kc, vc, cum))
    s_fin, y_inter = lax.scan(step, s0.astype(jnp.float32), xs)
    y = y_intra + jnp.moveaxis(y_inter, 0, 1)
    return y.reshape(b, t, h, dv).astype(v.dtype), s_fin


def _run_dir(scan_fn, args, s0, reverse):
    if reverse:
        args = tuple(jnp.flip(a, 1) for a in args)
    y, s = scan_fn(*args, s0)
    return (jnp.flip(y, 1) if reverse else y), s


def _token_mixers(h, lp, s_ssm, s_hg, s_ret, pos):
    b, t, _ = h.shape
    dt_h = h.dtype
    proj = jnp.einsum('btd,dc->btc', h, lp['w_in'])
    z, xbc, dt_raw, hq, hf, hi, hgate, rq, rk, rv, rgate, bgate = _split(proj, IN_SIZES)

    xbc = jax.nn.silu(_dwconv_centered(xbc, lp['m_conv_w'], lp['m_conv_b']))
    xm, bm, cm = _split(xbc, (M_INNER, M_GROUPS * M_STATE, M_GROUPS * M_STATE))
    xm = xm.reshape(b, t, M_GROUPS, M_HPG, M_HEADDIM)
    bm = bm.reshape(b, t, M_GROUPS, M_STATE)
    cm = cm.reshape(b, t, M_GROUPS, M_STATE)
    dt = jax.nn.softplus(dt_raw.astype(jnp.float32).reshape(b, t, 2, M_HEADS) + lp['m_dt_bias'])
    log_a = dt * (-jnp.exp(lp['m_a_log'].astype(jnp.float32)))
    dt = dt.reshape(b, t, 2, M_GROUPS, M_HPG)
    log_a = log_a.reshape(b, t, 2, M_GROUPS, M_HPG)
    ssm0 = s_ssm.reshape(b, 2, M_GROUPS, M_HPG, M_STATE, M_HEADDIM)
    y_f, s_f = _run_dir(_ssd_chunk_scan, (cm, bm, xm * dt[:, :, 0, ..., None], log_a[:, :, 0]), ssm0[:, 0], False)
    y_b, s_b = _run_dir(_ssd_chunk_scan, (cm, bm, xm * dt[:, :, 1, ..., None], log_a[:, :, 1]), ssm0[:, 1], True)
    y = y_f + y_b + lp['m_d'].reshape(M_GROUPS, M_HPG, 1) * xm
    y_ssm = _group_rms(y.reshape(b, t, M_INNER) * jax.nn.silu(z), lp['m_norm'], M_GROUPS)
    new_ssm = jnp.stack([s_f, s_b], 1).reshape(b, 2, M_HEADS, M_STATE, M_HEADDIM)

    lb = lp['hg_lb']
    log_f = jnp.logaddexp(jnp.log(lb), jnp.log1p(-lb) + jax.nn.log_sigmoid(hf.astype(jnp.float32).reshape(b, t, 2, HG_WIDTH)))
    k_hg = -jnp.expm1(log_f)
    q_hg = jax.nn.silu(hq).reshape(b, t, HG_HEADS, HG_DK)
    v_hg = hi.reshape(b, t, HG_HEADS, HG_DV)
    hshape = (b, t, HG_HEADS, HG_DK)
    o_f, g_f = _run_dir(_gla_chunk_scan, (q_hg, k_hg[:, :, 0].reshape(hshape), v_hg, log_f[:, :, 0].reshape(hshape)), s_hg[:, 0], False)
    o_b, g_b = _run_dir(_gla_chunk_scan, (q_hg, k_hg[:, :, 1].reshape(hshape), v_hg, log_f[:, :, 1].reshape(hshape)), s_hg[:, 1], True)
    y_hg = _group_rms((o_f + o_b).reshape(b, t, HG_WIDTH), lp['hg_norm'], HG_HEADS) * jax.nn.silu(hgate)
    new_hg = jnp.stack([g_f, g_b], 1)

    q_r = rq.reshape(b, t, RET_HEADS, RET_DK)
    k_r = rk.reshape(b, t, RET_HEADS, RET_DK) * (RET_DK ** -0.5)
    if pos is not None:
        q_r = _rope_2d(q_r, pos[0], pos[1])
        k_r = _rope_2d(k_r, pos[0], pos[1])
    v_r = rv.reshape(b, t, RET_HEADS, 1, RET_DV)
    log_g = jax.nn.log_sigmoid(lp['ret_decay'].astype(jnp.float32))
    la_f = jnp.broadcast_to(log_g[0][:, None], (b, t, RET_HEADS, 1))
    la_b = jnp.broadcast_to(log_g[1][:, None], (b, t, RET_HEADS, 1))
    r_f, t_f = _run_dir(_ssd_chunk_scan, (q_r, k_r, v_r, la_f), s_ret[:, 0][:, :, None], False)
    r_b, t_b = _run_dir(_ssd_chunk_scan, (q_r, k_r, v_r, la_b), s_ret[:, 1][:, :, None], True)
    y_ret = _group_rms((r_f + r_b).reshape(b, t, RET_WIDTH), lp['ret_norm'], RET_HEADS) * jax.nn.silu(rgate)
    new_ret = jnp.stack([t_f[:, :, 0], t_b[:, :, 0]], 1)

    g_s, g_h, g_r = jnp.split(jax.nn.sigmoid(bgate), N_BRANCH, axis=-1)
    merged = (g_s * (y_ssm.astype(dt_h) @ lp['w_br_ssm'])
              + g_h * (y_hg.astype(dt_h) @ lp['w_br_hg'])
              + g_r * (y_ret.astype(dt_h) @ lp['w_br_ret']))
    return merged @ lp['w_out'], (new_ssm, new_hg, new_ret)


def _peer(h, w_q, keys, u_tab, v_tab):
    b, t, d = h.shape
    n = b * t
    xt = h.reshape(n, d)
    q = (xt @ w_q).reshape(n, PK_HEADS, 2, PK_HALF)
    s = jnp.einsum('nhcd,hckd->nhck', q, keys).astype(jnp.float32)
    s1, i1 = lax.top_k(s[:, :, 0], PK_TOPK)
    s2, i2 = lax.top_k(s[:, :, 1], PK_TOPK)
    cand = (s1[..., :, None] + s2[..., None, :]).reshape(n, PK_HEADS, PK_TOPK * PK_TOPK)
    cidx = (i1[..., :, None] * N_KEYS + i2[..., None, :]).reshape(n, PK_HEADS, PK_TOPK * PK_TOPK)
    top, sel = lax.top_k(cand, PK_TOPK)
    idx = jnp.take_along_axis(cidx, sel, axis=-1)
    gate = jax.nn.softmax(top, axis=-1)
    nb = n // PEER_BLOCK

    def block(args):
        xb, ib, gb = args
        act = jax.nn.gelu(jnp.einsum('nd,nhkd->nhk', xb, u_tab[ib]).astype(jnp.float32), approximate=False) * gb
        return jnp.einsum('nhk,nhkd->nd', act.astype(xb.dtype), v_tab[ib])

    y = lax.map(block, (xt.reshape(nb, PEER_BLOCK, d), idx.reshape(nb, PEER_BLOCK, PK_HEADS, PK_TOPK), gate.reshape(nb, PEER_BLOCK, PK_HEADS, PK_TOPK)))
    return y.reshape(b, t, d)


def _layer(x, mod, lp, s_ssm, s_hg, s_ret, pos):
    shift1, scale1, gate1, shift2, scale2, gate2 = jnp.split(mod, 6, axis=-1)
    h = x * (1 + scale1) + shift1
    mix, states = _token_mixers(h, lp, s_ssm, s_hg, s_ret, pos)
    x = _layer_norm(ALPHA * x + gate1 * mix, lp['ln1_g'], lp['ln1_b'])
    h = x * (1 + scale2) + shift2
    ff = _peer(h, lp['pk_query'], lp['pk_keys'], lp['peer_u'], lp['peer_v'])
    x = _layer_norm(ALPHA * x + gate2 * ff, lp['ln2_g'], lp['ln2_b'])
    return x, states


def setup_inputs(seed: int = 0) -> dict:
    key = jax.random.key(seed)
    ks = jax.random.split(key, 40)
    f32 = jnp.float32

    def nrm(k, shape, std):
        return jax.random.normal(k, shape, f32) * std

    dt0 = jnp.exp(jax.random.uniform(ks[12], (DEPTH, 2, M_HEADS), f32, math.log(1e-3), math.log(1e-1)))
    gam = 2.0 ** (-5.0 - jnp.arange(RET_HEADS, dtype=f32))
    return {
        'x_prompt': nrm(ks[0], (BATCH, SEQ, D_MODEL), 1.0),
        'x_sample': nrm(ks[1], (DEC_BATCH, DEC_SEQ, D_MODEL), 1.0),
        'c': nrm(ks[2], (DEC_BATCH, D_MODEL), 1.0),
        'state_ssm': nrm(ks[3], (DEC_BATCH, DEPTH, 2, M_HEADS, M_STATE, M_HEADDIM), 0.5),
        'state_hgrn': nrm(ks[4], (DEC_BATCH, DEPTH, 2, HG_HEADS, HG_DK, HG_DV), 0.5),
        'state_ret': nrm(ks[5], (DEC_BATCH, DEPTH, 2, RET_HEADS, RET_DK, RET_DV), 0.5),
        'c_ctx': nrm(ks[6], (D_MODEL,), 1.0),
        'w_ada': nrm(ks[7], (DEPTH, D_MODEL, 6 * D_MODEL), D_MODEL ** -0.5),
        'b_ada': nrm(ks[8], (DEPTH, 6 * D_MODEL), 0.02),
        'w_in': nrm(ks[9], (DEPTH, D_MODEL, IN_COLS), D_MODEL ** -0.5),
        'm_conv_w': nrm(ks[10], (DEPTH, CONV_W, M_CONV_DIM), CONV_W ** -0.5),
        'm_conv_b': nrm(ks[11], (DEPTH, M_CONV_DIM), 0.02),
        'm_dt_bias': dt0 + jnp.log(-jnp.expm1(-dt0)),
        'm_a_log': jnp.log(jax.random.uniform(ks[13], (DEPTH, 2, M_HEADS), f32, 1.0, 16.0)),
        'm_d': 1.0 + nrm(ks[14], (DEPTH, M_HEADS), 0.1),
        'm_norm': 1.0 + nrm(ks[15], (DEPTH, M_INNER), 0.02),
        'hg_lower_bounds': nrm(ks[16], (DEPTH, 2, HG_WIDTH), 0.1),
        'hg_norm': 1.0 + nrm(ks[17], (DEPTH, HG_WIDTH), 0.02),
        'ret_decay': (jnp.log1p(-gam) - jnp.log(gam)) + nrm(ks[18], (DEPTH, 2, RET_HEADS), 0.05),
        'ret_norm': 1.0 + nrm(ks[19], (DEPTH, RET_WIDTH), 0.02),
        'w_br_ssm': nrm(ks[20], (DEPTH, M_INNER, D_MODEL), M_INNER ** -0.5),
        'w_br_hg': nrm(ks[21], (DEPTH, HG_WIDTH, D_MODEL), HG_WIDTH ** -0.5),
        'w_br_ret': nrm(ks[22], (DEPTH, RET_WIDTH, D_MODEL), RET_WIDTH ** -0.5),
        'w_out': nrm(ks[23], (DEPTH, D_MODEL, D_MODEL), BETA * D_MODEL ** -0.5),
        'ln1_g': 1.0 + nrm(ks[24], (DEPTH, D_MODEL), 0.02),
        'ln1_b': nrm(ks[25], (DEPTH, D_MODEL), 0.02),
        'ln2_g': 1.0 + nrm(ks[26], (DEPTH, D_MODEL), 0.02),
        'ln2_b': nrm(ks[27], (DEPTH, D_MODEL), 0.02),
        'pk_query': nrm(ks[28], (DEPTH, D_MODEL, PK_HEADS * PK_DIM), D_MODEL ** -0.5),
        'pk_keys': nrm(ks[29], (DEPTH, PK_HEADS, 2, N_KEYS, PK_HALF), PK_HALF ** -0.5),
        'peer_u': nrm(ks[30], (DEPTH, N_EXPERTS, D_MODEL), D_MODEL ** -0.5),
        'peer_v': nrm(ks[31], (DEPTH, N_EXPERTS, D_MODEL), BETA),
    }


def reference(x_prompt, x_sample, c, state_ssm, state_hgrn, state_ret, c_ctx, w_ada, b_ada, w_in, m_conv_w, m_conv_b, m_dt_bias, m_a_log, m_d, m_norm, hg_lower_bounds, hg_norm, ret_decay, ret_norm, w_br_ssm, w_br_hg, w_br_ret, w_out, ln1_g, ln1_b, ln2_g, ln2_b, pk_query, pk_keys, peer_u, peer_v):
    bp = x_prompt.shape[0]
    rows = x_sample.shape[1] // GRID_W
    pos = (jnp.repeat(jnp.arange(rows), GRID_W), jnp.arange(rows * GRID_W) % GRID_W)
    p_lb = jax.nn.softmax(hg_lower_bounds.astype(jnp.float32), axis=0)
    lower = jnp.cumsum(p_lb, axis=0) - p_lb[:1]
    zero_ssm = jnp.zeros((bp, 2, M_HEADS, M_STATE, M_HEADDIM), x_prompt.dtype)
    zero_hg = jnp.zeros((bp, 2, HG_HEADS, HG_DK, HG_DV), x_prompt.dtype)
    zero_ret = jnp.zeros((bp, 2, RET_HEADS, RET_DK, RET_DV), x_prompt.dtype)
    y_p = x_prompt
    y_s = x_sample
    out_ssm, out_hg, out_ret = [], [], []
    for l in range(DEPTH):
        lp = dict(w_in=w_in[l], m_conv_w=m_conv_w[l], m_conv_b=m_conv_b[l], m_dt_bias=m_dt_bias[l], m_a_log=m_a_log[l], m_d=m_d[l], m_norm=m_norm[l], hg_lb=lower[l], hg_norm=hg_norm[l], ret_decay=ret_decay[l], ret_norm=ret_norm[l], w_br_ssm=w_br_ssm[l], w_br_hg=w_br_hg[l], w_br_ret=w_br_ret[l], w_out=w_out[l], ln1_g=ln1_g[l], ln1_b=ln1_b[l], ln2_g=ln2_g[l], ln2_b=ln2_b[l], pk_query=pk_query[l], pk_keys=pk_keys[l], peer_u=peer_u[l], peer_v=peer_v[l])
        mod_ctx = (jax.nn.silu(c_ctx) @ w_ada[l] + b_ada[l])[None, None, :]
        mod_lat = (jax.nn.silu(c) @ w_ada[l] + b_ada[l])[:, None, :]
        y_p, (s_a, s_b, s_c) = _layer(y_p, mod_ctx, lp, zero_ssm, zero_hg, zero_ret, None)
        y_s, _ = _layer(y_s, mod_lat, lp, state_ssm[:, l], state_hgrn[:, l], state_ret[:, l], pos)
        out_ssm.append(s_a)
        out_hg.append(s_b)
        out_ret.append(s_c)
    new_state_ssm = jnp.stack(out_ssm, axis=1).astype(x_prompt.dtype)
    new_state_hgrn = jnp.stack(out_hg, axis=1).astype(x_prompt.dtype)
    new_state_ret = jnp.stack(out_ret, axis=1).astype(x_prompt.dtype)
    return (y_p, y_s, new_state_ssm, new_state_hgrn, new_state_ret)
```

```python
import functools
import math

import jax
import jax.numpy as jnp
from jax import lax
from jax.experimental import pallas as pl
from jax.experimental.pallas import tpu as pltpu

F32 = jnp.float32
BF16 = jnp.bfloat16

GRID_W = 64
M_GROUPS = 8
CHUNK = 64
HG_CHUNK = 32
PK_TOPK = 16
ROPE_BASE = 10000.0
EPS = 1e-6
N_MOD = 6

V7X_VMEM_BYTES = 64 * 2**20
VMEM_LIMIT = V7X_VMEM_BYTES - 8 * 2**20
LANES = 128
MASKED = -1e30


def _params(*sem):
    return pltpu.CompilerParams(dimension_semantics=sem, vmem_limit_bytes=VMEM_LIMIT)


def _silu(x):
    return x * jax.nn.sigmoid(x)


def _split3(a):
    hi = a.astype(BF16)
    r1 = a - hi.astype(F32)
    mid = r1.astype(BF16)
    lo = (r1 - mid.astype(F32)).astype(BF16)
    return hi, mid, lo


def _dot(a, b):
    return jnp.dot(a, b, preferred_element_type=F32)


def _dot_nt(a, b):
    return lax.dot_general(a, b, (((1,), (1,)), ((), ())), preferred_element_type=F32)


def _dot_tn(a, b):
    return lax.dot_general(a, b, (((0,), (0,)), ((), ())), preferred_element_type=F32)


def _exact_left(sel, a):
    hi, mid, lo = _split3(a)
    return _dot(sel, hi) + _dot(sel, mid) + _dot(sel, lo)


def _exact_right(a, sel):
    hi, mid, lo = _split3(a)
    return _dot(hi, sel) + _dot(mid, sel) + _dot(lo, sel)


class Geom:
    def __init__(self, n_ctx_seq, ctx_len, n_lat_seq, lat_len):
        self.n_ctx_seq, self.ctx_len, self.n_lat_seq, self.lat_len = n_ctx_seq, ctx_len, n_lat_seq, lat_len
        self.t_ctx = n_ctx_seq * ctx_len
        self.t_lat = n_lat_seq * lat_len
        self.t_all = self.t_ctx + self.t_lat
        self.rb = min(256, ctx_len)
        assert ctx_len % self.rb == 0 and lat_len % self.rb == 0
        self.ctx_blocks = self.t_ctx // self.rb
        self.ctx_bps = ctx_len // self.rb
        self.lat_bps = lat_len // self.rb

    def mod_row(self, i):
        return jnp.where(i < self.ctx_blocks, 0, 1 + (i - self.ctx_blocks) // self.lat_bps)


def _ada_kernel(c_ref, w_ref, b_ref, o_ref):
    a = _silu(c_ref[...]).astype(BF16)
    o_ref[...] = _dot(a, w_ref[...].astype(BF16)) + b_ref[...]


def _ada(cvec, w, b):
    r, d = cvec.shape
    n = w.shape[1]
    tn = _tile(n, 512)
    return pl.pallas_call(
        _ada_kernel, grid=(n // tn,),
        in_specs=[pl.BlockSpec((r, d), lambda j: (0, 0)),
                  pl.BlockSpec((d, tn), lambda j: (0, j)),
                  pl.BlockSpec((1, tn), lambda j: (0, j))],
        out_specs=pl.BlockSpec((r, tn), lambda j: (0, j)),
        out_shape=jax.ShapeDtypeStruct((r, n), F32),
        compiler_params=_params("parallel"), name="ada")(cvec, w, b.reshape(1, n))


def _ln_mod_kernel(*refs, alpha, has_ln, has_mod):
    refs = list(refs)
    x_ref = refs.pop(0)
    x = x_ref[...]
    if has_ln:
        mix_ref, gate_ref, g_ref, b_ref = refs[:4]
        refs = refs[4:]
    if has_mod:
        sc_ref, sh_ref = refs[:2]
        refs = refs[2:]
    if has_ln:
        y = alpha * x + gate_ref[0] * mix_ref[...]
        mu = jnp.mean(y, axis=-1, keepdims=True)
        yc = y - mu
        var = jnp.mean(yc * yc, axis=-1, keepdims=True)
        x = yc * lax.rsqrt(var + EPS) * g_ref[...] + b_ref[...]
        xo_ref = refs.pop(0)
        xo_ref[...] = x
    if has_mod:
        ho_ref = refs.pop(0)
        ho_ref[...] = (x * (1.0 + sc_ref[0]) + sh_ref[0]).astype(BF16)


def _ln_mod(geom, x, alpha, ln=None, mod=None):
    t, d = x.shape
    rb = geom.rb
    row = pl.BlockSpec((rb, d), lambda i: (i, 0))
    vec = pl.BlockSpec((1, d), lambda i: (0, 0))

    def slot(k):
        return pl.BlockSpec((1, 1, d), lambda i: (geom.mod_row(i) * N_MOD + k, 0, 0))

    args, specs, outs, ospecs = [x], [row], [], []
    if ln is not None:
        mix, tab, gk, g, b = ln
        args += [mix, tab, g.reshape(1, d), b.reshape(1, d)]
        specs += [row, slot(gk), vec, vec]
        outs.append(jax.ShapeDtypeStruct((t, d), F32))
        ospecs.append(row)
    if mod is not None:
        tab, sk, hk = mod
        args += [tab, tab]
        specs += [slot(sk), slot(hk)]
        outs.append(jax.ShapeDtypeStruct((t, d), BF16))
        ospecs.append(row)
    res = pl.pallas_call(
        functools.partial(_ln_mod_kernel, alpha=alpha, has_ln=ln is not None, has_mod=mod is not None),
        grid=(t // rb,), in_specs=specs, out_specs=ospecs, out_shape=outs,
        compiler_params=_params("parallel"), name="ln_mod")(*args)
    return res


def _mm_kernel(x_ref, w_ref, o_ref):
    o_ref[...] = _dot(x_ref[...], w_ref[...]).astype(o_ref.dtype)


def _tile(n, want):
    if n <= want:
        return n
    t = want
    while n % t:
        t //= 2
    return t


def _matmul(x, w, out_dtype, tm=1024, tn=512):
    m, k = x.shape
    n = w.shape[1]
    tm, tn = _tile(m, tm), _tile(n, tn)
    assert tm % 8 == 0 and tn % LANES == 0
    return pl.pallas_call(
        _mm_kernel, grid=(m // tm, n // tn),
        in_specs=[pl.BlockSpec((tm, k), lambda i, j: (i, 0)),
                  pl.BlockSpec((k, tn), lambda i, j: (0, j))],
        out_specs=pl.BlockSpec((tm, tn), lambda i, j: (i, j)),
        out_shape=jax.ShapeDtypeStruct((m, n), out_dtype),
        compiler_params=_params("parallel", "arbitrary"), name="matmul")(x, w)


CONV_HALO = 16


def _conv_kernel(prev_ref, cur_ref, next_ref, w_ref, b_ref, o_ref, *, geom, conv_w):
    i = pl.program_id(0)
    rb = geom.rb
    in_ctx = i < geom.ctx_blocks
    k_ctx = i % geom.ctx_bps
    k_lat = (i - geom.ctx_blocks) % geom.lat_bps
    is_start = jnp.where(in_ctx, k_ctx == 0, k_lat == 0)
    is_end = jnp.where(in_ctx, k_ctx == geom.ctx_bps - 1, k_lat == geom.lat_bps - 1)
    prev = jnp.where(is_start, 0.0, prev_ref[...].astype(F32))
    nxt = jnp.where(is_end, 0.0, next_ref[...].astype(F32))
    z = jnp.concatenate([prev, cur_ref[...].astype(F32), nxt], axis=0)
    rows = rb + 2 * CONV_HALO
    pad = conv_w // 2
    acc = jnp.zeros(o_ref.shape, F32) + b_ref[...]
    for k in range(conv_w):
        zk = z if k == pad else pltpu.roll(z, (pad - k) % rows, 0)
        acc = acc + w_ref[k:k + 1, :] * zk[CONV_HALO:CONV_HALO + rb]
    o_ref[...] = _silu(acc).astype(o_ref.dtype)


def _conv_silu(geom, x, w, b):
    t, c = x.shape
    rb = geom.rb
    cb = _tile(c, 512)
    assert rb % CONV_HALO == 0
    hb = rb // CONV_HALO
    last = t // CONV_HALO - 1
    conv_w = w.shape[0]
    return pl.pallas_call(
        functools.partial(_conv_kernel, geom=geom, conv_w=conv_w), grid=(t // rb, c // cb),
        in_specs=[pl.BlockSpec((CONV_HALO, cb), lambda i, j: (jnp.maximum(i * hb - 1, 0), j)),
                  pl.BlockSpec((rb, cb), lambda i, j: (i, j)),
                  pl.BlockSpec((CONV_HALO, cb), lambda i, j: (jnp.minimum((i + 1) * hb, last), j)),
                  pl.BlockSpec((conv_w, cb), lambda i, j: (0, j)),
                  pl.BlockSpec((1, cb), lambda i, j: (0, j))],
        out_specs=pl.BlockSpec((rb, cb), lambda i, j: (i, j)),
        out_shape=jax.ShapeDtypeStruct((t, c), BF16),
        compiler_params=_params("parallel", "parallel"), name="conv_silu")(x, x, x, w, b.reshape(1, c))


def _ssd_kernel(*refs, dims, has_init):
    h_all, g_all, n_state, p_dim = dims
    hpg = h_all // g_all
    inner = h_all * p_dim
    gw = hpg * p_dim
    L = CHUNK
    refs = list(refs)
    xc = refs[0:2]
    dt = refs[2:4]
    dtt = refs[4:6]
    brow, arow, bcol, acol, e_ref = refs[6:11]
    refs = refs[11:]
    if has_init:
        s0_ref = refs.pop(0)
    y_out = refs[0:2]
    refs = refs[2:]
    if not has_init:
        sfin_ref = refs.pop(0)
    s_scr, cumx_s, ecx_s, xdt_s, xs_s, cumt_s, elast_s = refs

    c = pl.program_id(1)

    @pl.when(c == 0)
    def _():
        if has_init:
            s_scr[...] = s0_ref[0]
        else:
            s_scr[...] = jnp.zeros(s_scr.shape, F32)

    ri = lax.broadcasted_iota(jnp.int32, (L, L), 0)
    ci = lax.broadcasted_iota(jnp.int32, (L, L), 1)
    e_mat = e_ref[...]

    for d in range(2):
        lo, hi = d * h_all, (d + 1) * h_all
        keep = (ci <= ri) if d == 0 else (ci >= ri)
        tri = keep.astype(BF16)
        dtv = jax.nn.softplus(dt[d][:, lo:hi] + brow[:, lo:hi])
        cum = _exact_left(tri, dtv * arow[:, lo:hi])
        cumx = _exact_right(cum, e_mat)
        dtx = _exact_right(dtv, e_mat)
        last = cumx[L - 1:L, :] if d == 0 else cumx[0:1, :]
        xdt = xc[d][:, :inner].astype(F32) * dtx
        cumx_s[...] = cumx
        ecx_s[...] = jnp.exp(cumx)
        xdt_s[...] = xdt.astype(BF16)
        xs_s[...] = (xdt * jnp.exp(last - cumx)).astype(BF16)
        elast_s[...] = jnp.broadcast_to(jnp.exp(last), elast_s.shape)
        lat = jax.nn.softplus(dtt[d][0, lo:hi, :] + bcol[lo:hi, :]) * acol[lo:hi, :]
        keep_t = (ri <= ci) if d == 0 else (ri >= ci)
        cumt_s[...] = _exact_right(lat, keep_t.astype(BF16))

        def group(g, carry, d=d, keep=keep):
            off_b = pl.multiple_of(inner + g * n_state, n_state)
            off_c = pl.multiple_of(inner + (g_all + g) * n_state, n_state)
            off_g = pl.multiple_of(g * gw, gw)
            bg = xc[d][:, pl.ds(off_b, n_state)]
            cg = xc[d][:, pl.ds(off_c, n_state)]
            scores = _dot_nt(cg, bg)
            s_old = s_scr[d, g]
            inter = _dot(cg, s_old.astype(BF16)) * ecx_s[:, pl.ds(off_g, gw)]
            cum_g = cumx_s[:, pl.ds(off_g, gw)]
            xdt_g = xdt_s[:, pl.ds(off_g, gw)]
            ys = []
            for h in range(hpg):
                col = cum_g[:, h * p_dim:h * p_dim + 1]
                row = cumt_s[pl.ds(g * hpg + h, 1), :]
                dec = jnp.exp(jnp.where(keep, col - row, -jnp.inf))
                ys.append(_dot((scores * dec).astype(BF16), xdt_g[:, h * p_dim:(h + 1) * p_dim]))
            y_out[d][:, pl.ds(off_g, gw)] = jnp.concatenate(ys, axis=1) + inter
            s_scr[d, g] = s_old * elast_s[0:1, pl.ds(off_g, gw)] + _dot_tn(bg, xs_s[:, pl.ds(off_g, gw)])
            return carry

        lax.fori_loop(0, g_all, group, 0)

    if not has_init:
        @pl.when(c == pl.num_programs(1) - 1)
        def _():
            sfin_ref[0] = s_scr[...]


def _ssd_scan(xc, dt, dtt, prm, s0, *, dims, n_seq, seq_len, row_off):
    h_all, g_all, n_state, p_dim = dims
    inner, gw = h_all * p_dim, (h_all // g_all) * p_dim
    L = CHUNK
    nc = seq_len // L
    off = row_off // L
    cw = xc.shape[1]
    has_init = s0 is not None
    brow, arow, bcol, acol, e_mat = prm

    def fwd(b, c):
        return off + b * nc + c

    def bwd(b, c):
        return off + b * nc + (nc - 1 - c)

    full = lambda a: pl.BlockSpec(a.shape, lambda b, c: (0,) * a.ndim)
    in_specs = [pl.BlockSpec((L, cw), lambda b, c: (fwd(b, c), 0)),
                pl.BlockSpec((L, cw), lambda b, c: (bwd(b, c), 0)),
                pl.BlockSpec((L, 2 * h_all), lambda b, c: (fwd(b, c), 0)),
                pl.BlockSpec((L, 2 * h_all), lambda b, c: (bwd(b, c), 0)),
                pl.BlockSpec((1, 2 * h_all, L), lambda b, c: (fwd(b, c), 0, 0)),
                pl.BlockSpec((1, 2 * h_all, L), lambda b, c: (bwd(b, c), 0, 0)),
                full(brow), full(arow), full(bcol), full(acol), full(e_mat)]
    args = [xc, xc, dt, dt, dtt, dtt, brow, arow, bcol, acol, e_mat]
    st_block = (1, 2, g_all, n_state, gw)
    if has_init:
        in_specs.append(pl.BlockSpec(st_block, lambda b, c: (b, 0, 0, 0, 0)))
        args.append(s0)
    t_loc = n_seq * seq_len
    out_shape = [jax.ShapeDtypeStruct((t_loc, inner), F32)] * 2
    out_specs = [pl.BlockSpec((L, inner), lambda b, c: (b * nc + c, 0)),
                 pl.BlockSpec((L, inner), lambda b, c: (b * nc + (nc - 1 - c), 0))]
    if not has_init:
        out_shape.append(jax.ShapeDtypeStruct((n_seq,) + st_block[1:], F32))
        out_specs.append(pl.BlockSpec(st_block, lambda b, c: (b, 0, 0, 0, 0)))
    scratch = [pltpu.VMEM((2, g_all, n_state, gw), F32),
               pltpu.VMEM((L, inner), F32), pltpu.VMEM((L, inner), F32),
               pltpu.VMEM((L, inner), BF16), pltpu.VMEM((L, inner), BF16),
               pltpu.VMEM((h_all, L), F32), pltpu.VMEM((8, inner), F32)]
    return pl.pallas_call(
        functools.partial(_ssd_kernel, dims=dims, has_init=has_init), grid=(n_seq, nc),
        in_specs=in_specs, out_specs=out_specs, out_shape=out_shape, scratch_shapes=scratch,
        compiler_params=_params("parallel", "arbitrary"), name="ssd_scan")(*args)


def _gla_kernel(*refs, tb, has_init):
    L = HG_CHUNK
    nch = tb // L
    mid = L // 2
    refs = list(refs)
    q_r, v_r, f_r = refs[0:2], refs[2:4], refs[4:6]
    loglb, log1m, onem = refs[6:8], refs[8:10], refs[10:12]
    refs = refs[12:]
    if has_init:
        s0_ref = refs.pop(0)
    o_r = refs[0:2]
    refs = refs[2:]
    if not has_init:
        sfin_ref = refs.pop(0)
    st_scr, = refs

    tblk = pl.program_id(2)

    @pl.when(tblk == 0)
    def _():
        if has_init:
            st_scr[...] = s0_ref[0, :, 0]
        else:
            st_scr[...] = jnp.zeros(st_scr.shape, F32)

    ri = lax.broadcasted_iota(jnp.int32, (tb, tb), 0)
    ci = lax.broadcasted_iota(jnp.int32, (tb, tb), 1)
    same = (ri // L) == (ci // L)
    rl = lax.broadcasted_iota(jnp.int32, (L, L), 0)
    cl = lax.broadcasted_iota(jnp.int32, (L, L), 1)

    for d in range(2):
        hf = f_r[d][...]
        log_sig = jnp.minimum(hf, 0.0) - jnp.log1p(jnp.exp(-jnp.abs(hf)))
        a = loglb[d][0]
        b = log1m[d][0] + log_sig
        log_f = jnp.maximum(a, b) + jnp.log1p(jnp.exp(-jnp.abs(a - b)))
        k = onem[d][0] * jax.nn.sigmoid(-hf)
        q = _silu(q_r[d][...].astype(F32))
        v = v_r[d][...]
        tri = (same & ((ci <= ri) if d == 0 else (ci >= ri))).astype(BF16)
        cum = _exact_left(tri, log_f)
        ref_row = mid if d == 0 else mid - 1
        last_row = L - 1 if d == 0 else 0
        refx = jnp.concatenate([jnp.broadcast_to(cum[j * L + ref_row:j * L + ref_row + 1], (L, cum.shape[1]))
                                for j in range(nch)], axis=0)
        lastx = jnp.concatenate([jnp.broadcast_to(cum[j * L + last_row:j * L + last_row + 1], (L, cum.shape[1]))
                                 for j in range(nch)], axis=0)
        qd = (q * jnp.exp(cum - refx)).astype(BF16)
        kd = (k * jnp.exp(refx - cum)).astype(BF16)
        qs = (q * jnp.exp(cum)).astype(BF16)
        ks = (k * jnp.exp(lastx - cum)).astype(BF16)
        elast = jnp.exp(lastx)
        keep = (cl <= rl) if d == 0 else (cl >= rl)
        st = st_scr[d]
        order = range(nch) if d == 0 else range(nch - 1, -1, -1)
        for j in order:
            sl = slice(j * L, (j + 1) * L)
            sc = jnp.where(keep, _dot_nt(qd[sl], kd[sl]), 0.0)
            y = _dot(sc.astype(BF16), v[sl]) + _dot_nt(qs[sl], st.astype(BF16))
            o_r[d][sl, :] = y
            st = st * elast[j * L:j * L + 1] + _dot_tn(v[sl], ks[sl])
        st_scr[d] = st

    if not has_init:
        @pl.when(tblk == pl.num_programs(2) - 1)
        def _():
            sfin_ref[0, :, 0] = st_scr[...]


def _gla_scan(hq, hi, hf, prm, s0, *, heads, dk, dv, n_seq, seq_len, row_off):
    tb = min(256, seq_len)
    ntb = seq_len // tb
    off = row_off // tb
    has_init = s0 is not None
    loglb, log1m, onem = prm

    def fwd(b, h, t):
        return off + b * ntb + t

    def bwd(b, h, t):
        return off + b * ntb + (ntb - 1 - t)

    def tok(rowf, col0):
        return pl.BlockSpec((tb, dk), lambda b, h, t: (rowf(b, h, t), col0 + h))

    def par(d):
        return pl.BlockSpec((1, 1, dk), lambda b, h, t: (d, 0, h))

    in_specs = [tok(fwd, 0), tok(bwd, 0), tok(fwd, 0), tok(bwd, 0), tok(fwd, 0), tok(bwd, heads),
                par(0), par(1), par(0), par(1), par(0), par(1)]
    args = [hq, hq, hi, hi, hf, hf, loglb, loglb, log1m, log1m, onem, onem]
    st_block = (1, 2, 1, dv, dk)
    st_map = lambda b, h, t: (b, 0, h, 0, 0)
    if has_init:
        in_specs.append(pl.BlockSpec(st_block, st_map))
        args.append(s0)
    t_loc = n_seq * seq_len
    out_shape = [jax.ShapeDtypeStruct((t_loc, heads * dv), F32)] * 2
    out_specs = [pl.BlockSpec((tb, dv), lambda b, h, t: (b * ntb + t, h)),
                 pl.BlockSpec((tb, dv), lambda b, h, t: (b * ntb + (ntb - 1 - t), h))]
    if not has_init:
        out_shape.append(jax.ShapeDtypeStruct((n_seq, 2, heads, dv, dk), F32))
        out_specs.append(pl.BlockSpec(st_block, st_map))
    return pl.pallas_call(
        functools.partial(_gla_kernel, tb=tb, has_init=has_init), grid=(n_seq, heads, ntb),
        in_specs=in_specs, out_specs=out_specs, out_shape=out_shape,
        scratch_shapes=[pltpu.VMEM((2, dv, dk), F32)],
        compiler_params=_params("parallel", "parallel", "arbitrary"), name="gla_scan")(*args)


def _ret_kernel(*refs, tb, dk, has_init, scale):
    L = CHUNK
    nch = tb // L
    refs = list(refs)
    q_r, k_r, v_r = refs[0:2], refs[2:4], refs[4:6]
    lg_r = refs[6:8]
    refs = refs[8:]
    if has_init:
        cos_r, sin_r = refs[0:2], refs[2:4]
        s0_ref = refs[4]
        refs = refs[5:]
    o_r = refs[0:2]
    refs = refs[2:]
    if not has_init:
        sfin_ref = refs.pop(0)
    s_scr, = refs

    tblk = pl.program_id(2)

    @pl.when(tblk == 0)
    def _():
        if has_init:
            s_scr[...] = s0_ref[0, :, 0]
        else:
            s_scr[...] = jnp.zeros(s_scr.shape, F32)

    rl = lax.broadcasted_iota(jnp.int32, (L, L), 0)
    cl = lax.broadcasted_iota(jnp.int32, (L, L), 1)
    pos = lax.broadcasted_iota(jnp.int32, (L, dk), 0).astype(F32)

    def rope(x, cos, sin):
        parts = [pltpu.roll(x[:, i:i + LANES], LANES // 2, 1) for i in range(0, dk, LANES)]
        return x * cos + jnp.concatenate(parts, axis=1) * sin

    for d in range(2):
        lg = lg_r[d][0, 0]
        lgk = jnp.concatenate([lg] * (dk // LANES), axis=1)
        q = q_r[d][...].astype(F32)
        k = k_r[d][...].astype(F32) * scale
        if has_init:
            q = rope(q, cos_r[d][...], sin_r[d][...])
            k = rope(k, cos_r[d][...], sin_r[d][...])
        q = q.astype(BF16)
        k = k.astype(BF16)
        v = v_r[d][...]
        if d == 0:
            keep = cl <= rl
            dmat = jnp.exp(jnp.where(keep, (rl - cl).astype(F32) * lg[:, :L], -jnp.inf))
            w_q = jnp.exp((pos + 1.0) * lgk)
            w_k = jnp.exp((L - 1.0 - pos) * lgk)
        else:
            keep = cl >= rl
            dmat = jnp.exp(jnp.where(keep, (cl - rl).astype(F32) * lg[:, :L], -jnp.inf))
            w_q = jnp.exp((L - pos) * lgk)
            w_k = jnp.exp(pos * lgk)
        e_chunk = jnp.exp(float(L) * lgk)
        s = s_scr[d]
        order = range(nch) if d == 0 else range(nch - 1, -1, -1)
        for j in order:
            sl = slice(j * L, (j + 1) * L)
            qj, kj, vj = q[sl], k[sl], v[sl]
            sc = _dot_nt(qj, kj) * dmat
            y = _dot(sc.astype(BF16), vj) + _dot(qj, s.astype(BF16)) * w_q[:, :v.shape[1]]
            o_r[d][sl, :] = y
            kw = (kj.astype(F32) * w_k).astype(BF16)
            s = s * e_chunk[:, :1] + _dot_tn(kw, vj)
        s_scr[d] = s

    if not has_init:
        @pl.when(tblk == pl.num_programs(2) - 1)
        def _():
            sfin_ref[0, :, 0] = s_scr[...]


def _ret_scan(rq, rk, rv, lg, rope, s0, *, heads, dk, dv, n_seq, seq_len, row_off):
    assert dk == dv
    tb = min(256, seq_len)
    ntb = seq_len // tb
    off = row_off // tb
    has_init = s0 is not None

    def fwd(b, h, t):
        return off + b * ntb + t

    def bwd(b, h, t):
        return off + b * ntb + (ntb - 1 - t)

    def tok(rowf):
        return pl.BlockSpec((tb, dk), lambda b, h, t: (rowf(b, h, t), h))

    def lgs(d):
        return pl.BlockSpec((1, 1, 1, LANES), lambda b, h, t: (d, h, 0, 0))

    in_specs = [tok(fwd), tok(bwd), tok(fwd), tok(bwd), tok(fwd), tok(bwd), lgs(0), lgs(1)]
    args = [rq, rq, rk, rk, rv, rv, lg, lg]
    st_block = (1, 2, 1, dk, dv)
    st_map = lambda b, h, t: (b, 0, h, 0, 0)
    if has_init:
        cos, sin = rope
        pf = pl.BlockSpec((tb, dk), lambda b, h, t: (t, 0))
        pb = pl.BlockSpec((tb, dk), lambda b, h, t: (ntb - 1 - t, 0))
        in_specs += [pf, pb, pf, pb, pl.BlockSpec(st_block, st_map)]
        args += [cos, cos, sin, sin, s0]
    t_loc = n_seq * seq_len
    out_shape = [jax.ShapeDtypeStruct((t_loc, heads * dv), F32)] * 2
    out_specs = [pl.BlockSpec((tb, dv), lambda b, h, t: (b * ntb + t, h)),
                 pl.BlockSpec((tb, dv), lambda b, h, t: (b * ntb + (ntb - 1 - t), h))]
    if not has_init:
        out_shape.append(jax.ShapeDtypeStruct((n_seq, 2, heads, dk, dv), F32))
        out_specs.append(pl.BlockSpec(st_block, st_map))
    return pl.pallas_call(
        functools.partial(_ret_kernel, tb=tb, dk=dk, has_init=has_init, scale=dk ** -0.5),
        grid=(n_seq, heads, ntb),
        in_specs=in_specs, out_specs=out_specs, out_shape=out_shape,
        scratch_shapes=[pltpu.VMEM((2, dk, dv), F32)],
        compiler_params=_params("parallel", "parallel", "arbitrary"), name="ret_scan")(*args)


def _grms_kernel(*refs, gs, gate_before, has_skip):
    refs = list(refs)
    a_ref, b_ref, gate_ref, w_ref = refs[:4]
    refs = refs[4:]
    if has_skip:
        x_ref, d_ref = refs[:2]
        refs = refs[2:]
    o_ref, = refs
    y = a_ref[...] + b_ref[...]
    if has_skip:
        y = y + d_ref[...] * x_ref[...].astype(F32)
    gate = _silu(gate_ref[...].astype(F32))
    if gate_before:
        y = y * gate
    width = y.shape[1]
    outs = []
    for g in range(width // gs):
        yg = y[:, g * gs:(g + 1) * gs]
        ms = jnp.mean(yg * yg, axis=-1, keepdims=True)
        outs.append(yg * lax.rsqrt(ms + EPS))
    yn = jnp.concatenate(outs, axis=1) * w_ref[...]
    if not gate_before:
        yn = yn * gate
    o_ref[...] = yn.astype(o_ref.dtype)


def _group_rms_gate(geom, a, b, gate, gate_col0, w, *, gs, gate_before, skip=None):
    t, width = a.shape
    rb = geom.rb
    cb = max(_tile(width, 512), gs)
    assert width % cb == 0 and cb % gs == 0 and gate_col0 % cb == 0
    row = pl.BlockSpec((rb, cb), lambda i, j: (i, j))
    vec = pl.BlockSpec((1, cb), lambda i, j: (0, j))
    g0 = gate_col0 // cb
    in_specs = [row, row, pl.BlockSpec((rb, cb), lambda i, j: (i, g0 + j)), vec]
    args = [a, b, gate, w.reshape(1, width)]
    if skip is not None:
        x, dvec = skip
        in_specs += [row, vec]
        args += [x, dvec.reshape(1, width)]
    return pl.pallas_call(
        functools.partial(_grms_kernel, gs=gs, gate_before=gate_before, has_skip=skip is not None),
        grid=(t // rb, width // cb), in_specs=in_specs, out_specs=row,
        out_shape=jax.ShapeDtypeStruct((t, width), BF16),
        compiler_params=_params("parallel", "parallel"), name="group_rms_gate")(*args)


def _merge_kernel(gs_ref, gh_ref, gr_ref, ps_ref, ph_ref, pr_ref, o_ref):
    acc = jax.nn.sigmoid(gs_ref[...].astype(F32)) * ps_ref[...]
    acc = acc + jax.nn.sigmoid(gh_ref[...].astype(F32)) * ph_ref[...]
    acc = acc + jax.nn.sigmoid(gr_ref[...].astype(F32)) * pr_ref[...]
    o_ref[...] = acc.astype(o_ref.dtype)


def _merge(geom, bgate, ps, ph, pr):
    t, d = ps.shape
    rb = geom.rb
    cb = _tile(d, 1024)
    nb = d // cb
    row = pl.BlockSpec((rb, cb), lambda i, j: (i, j))
    gate = lambda k: pl.BlockSpec((rb, cb), lambda i, j: (i, k * nb + j))
    return pl.pallas_call(
        _merge_kernel, grid=(t // rb, nb),
        in_specs=[gate(0), gate(1), gate(2), row, row, row],
        out_specs=row, out_shape=jax.ShapeDtypeStruct((t, d), BF16),
        compiler_params=_params("parallel", "parallel"), name="merge")(bgate, bgate, bgate, ps, ph, pr)


def _topk_desc(s, k):
    cols = []
    cur = s
    for _ in range(k):
        m = jnp.max(cur, axis=-1, keepdims=True)
        cols.append(m)
        cur = jnp.where(cur == m, -jnp.inf, cur)
    return cols


def _route_kernel(q_ref, keys_ref, a1_ref, s1_ref, a2_ref, s2_ref, tau_ref, *, heads, nk, half, topk):
    tm = q_ref.shape[0]
    lane_c = lax.broadcasted_iota(jnp.int32, (tm, topk * topk), 1)
    lane_t = lax.broadcasted_iota(jnp.int32, (tm, LANES), 1)
    tau_all = jnp.zeros((tm, LANES), F32)
    for h in range(heads):
        tops, masked = [], []
        for c in range(2):
            qh = q_ref[:, (2 * h + c) * half:(2 * h + c + 1) * half].astype(BF16)
            s = _dot_nt(qh, keys_ref[h, c])
            cols = _topk_desc(s, topk)
            tops.append(cols)
            masked.append(jnp.where(s >= cols[-1], s, MASKED))
        v1 = jnp.zeros((tm, topk * topk), F32)
        v2 = jnp.zeros((tm, topk * topk), F32)
        for a in range(topk):
            v1 = jnp.where(lane_c // topk == a, tops[0][a], v1)
            v2 = jnp.where(lane_c % topk == a, tops[1][a], v2)
        best = _topk_desc(v1 + v2, topk)
        tau = best[-1]
        z = sum(jnp.exp(b - best[0]) for b in best)
        inv_z = 1.0 / z
        a1_ref[:, h * nk:(h + 1) * nk] = jnp.exp(masked[0] - tops[0][0]) * inv_z
        a2_ref[:, h * nk:(h + 1) * nk] = jnp.exp(masked[1] - tops[1][0])
        s1_ref[:, h * nk:(h + 1) * nk] = masked[0]
        s2_ref[:, h * nk:(h + 1) * nk] = masked[1]
        tau_all = jnp.where(lane_t == h, tau, tau_all)
    tau_ref[...] = tau_all


def _route(q, keys, *, topk):
    t = q.shape[0]
    heads, _, nk, half = keys.shape
    assert nk == LANES and heads <= LANES
    tm = _tile(t, 256)
    row = lambda w: pl.BlockSpec((tm, w), lambda i: (i, 0))
    wide = jax.ShapeDtypeStruct((t, heads * nk), F32)
    return pl.pallas_call(
        functools.partial(_route_kernel, heads=heads, nk=nk, half=half, topk=topk), grid=(t // tm,),
        in_specs=[row(q.shape[1]), pl.BlockSpec(keys.shape, lambda i: (0, 0, 0, 0))],
        out_specs=[row(heads * nk)] * 4 + [row(LANES)],
        out_shape=[wide] * 4 + [jax.ShapeDtypeStruct((t, LANES), F32)],
        compiler_params=_params("parallel"), name="peer_route")(q, keys)


def _peer_kernel(x_ref, u_ref, v_ref, a1_ref, s1_ref, a2_ref, s2_ref, tau_ref, o_ref, *, heads, nk):
    j = pl.program_id(1)
    tn = u_ref.shape[0]
    pre = _dot_nt(x_ref[...], u_ref[...])
    act = 0.5 * pre * (1.0 + lax.erf(pre * (1.0 / math.sqrt(2.0))))
    a1 = a1_ref[0]
    s1 = s1_ref[0]
    tau = tau_ref[...]
    blocks = []
    for e in range(tn // nk):
        gate = jnp.zeros((x_ref.shape[0], nk), F32)
        for h in range(heads):
            col = e * heads + h
            hit = (s1[:, col:col + 1] + s2_ref[:, h * nk:(h + 1) * nk]) >= tau[:, h:h + 1]
            gate = gate + jnp.where(hit, a1[:, col:col + 1] * a2_ref[:, h * nk:(h + 1) * nk], 0.0)
        blocks.append(act[:, e * nk:(e + 1) * nk] * gate)
    w = jnp.concatenate(blocks, axis=1).astype(BF16)
    contrib = _dot(w, v_ref[...])

    @pl.when(j == 0)
    def _():
        o_ref[...] = contrib

    @pl.when(j > 0)
    def _():
        o_ref[...] += contrib


def _peer(x, u, v, route, *, heads, nk, tm=512, tn=256):
    t, d = x.shape
    e_all = u.shape[0]
    tm, tn = _tile(t, tm), _tile(e_all, tn)
    assert tn % nk == 0
    a1, s1, a2, s2, tau = route
    ne = tn // nk
    nt = e_all // tn
    assert ne * heads <= LANES

    def per_tile(a):
        a = a.reshape(t, heads, nt, ne).transpose(2, 0, 3, 1).reshape(nt, t, ne * heads)
        return jnp.pad(a, ((0, 0), (0, 0), (0, LANES - ne * heads)))

    rowx = pl.BlockSpec((tm, d), lambda i, j: (i, 0))
    tab = pl.BlockSpec((tn, d), lambda i, j: (j, 0))
    tile = pl.BlockSpec((1, tm, LANES), lambda i, j: (j, i, 0))
    wide = pl.BlockSpec((tm, heads * nk), lambda i, j: (i, 0))
    return pl.pallas_call(
        functools.partial(_peer_kernel, heads=heads, nk=nk), grid=(t // tm, nt),
        in_specs=[rowx, tab, tab, tile, tile, wide, wide, pl.BlockSpec((tm, LANES), lambda i, j: (i, 0))],
        out_specs=rowx, out_shape=jax.ShapeDtypeStruct((t, d), F32),
        compiler_params=_params("parallel", "arbitrary"), name="peer_dense")(
            x, u, v, per_tile(a1), per_tile(s1), a2, s2, tau)


def _rope_tables(seq_len, dk):
    half, quarter = dk // 2, dk // 4
    freqs = ROPE_BASE ** (-jnp.arange(quarter, dtype=F32) / quarter)
    t = jnp.arange(seq_len)
    ang_r = (t // GRID_W).astype(F32)[:, None] * freqs
    ang_c = (t % GRID_W).astype(F32)[:, None] * freqs
    cos = jnp.concatenate([jnp.cos(ang_r)] * 2 + [jnp.cos(ang_c)] * 2, axis=1)
    sin = jnp.concatenate([-jnp.sin(ang_r), jnp.sin(ang_r), -jnp.sin(ang_c), jnp.sin(ang_c)], axis=1)
    return cos, sin


def kernel(x_prompt, x_sample, c, state_ssm, state_hgrn, state_ret, c_ctx, w_ada, b_ada, w_in, m_conv_w, m_conv_b, m_dt_bias, m_a_log, m_d, m_norm, hg_lower_bounds, hg_norm, ret_decay, ret_norm, w_br_ssm, w_br_hg, w_br_ret, w_out, ln1_g, ln1_b, ln2_g, ln2_b, pk_query, pk_keys, peer_u, peer_v):
    bp, seq, d_model = x_prompt.shape
    bl, dec_seq, _ = x_sample.shape
    depth = w_in.shape[0]
    m_heads, m_state, m_headdim = state_ssm.shape[3:]
    hg_heads, hg_dk, hg_dv = state_hgrn.shape[3:]
    ret_heads, ret_dk, ret_dv = state_ret.shape[3:]
    m_inner = m_heads * m_headdim
    m_bc = M_GROUPS * m_state
    hg_w = hg_heads * hg_dk
    ret_w = ret_heads * ret_dk
    pk_heads, _, n_keys, pk_half = pk_keys.shape[1:]
    hpg = m_heads // M_GROUPS
    gw = hpg * m_headdim
    alpha = (2 * depth) ** 0.25
    geom = Geom(bp, seq, bl, dec_seq)
    t_ctx = geom.t_ctx
    ssd_dims = (m_heads, M_GROUPS, m_state, m_headdim)

    sizes = (m_inner, m_inner + 2 * m_bc, 2 * m_heads, hg_w, 2 * hg_w, hg_w, hg_w, ret_w, ret_w, ret_w, ret_w, 3 * d_model)
    offs = [0]
    for s in sizes:
        offs.append(offs[-1] + s)
    seg_dtype = (BF16, BF16, F32, BF16, F32, BF16, BF16, BF16, BF16, BF16, BF16, BF16)

    x = jnp.concatenate([x_prompt.reshape(t_ctx, d_model), x_sample.reshape(geom.t_lat, d_model)], axis=0)
    rows = 8 * ((1 + bl + 7) // 8)
    cvec = jnp.zeros((rows, d_model), F32).at[0].set(c_ctx).at[1:1 + bl].set(c)

    p_lb = jax.nn.softmax(hg_lower_bounds.astype(F32), axis=0)
    lower = jnp.cumsum(p_lb, axis=0) - p_lb[:1]
    rope = _rope_tables(dec_seq, ret_dk)
    e_mat = jnp.repeat(jnp.eye(m_heads, dtype=BF16), m_headdim, axis=1)
    log_g = jax.nn.log_sigmoid(ret_decay.astype(F32))

    out_ssm, out_hg, out_ret = [], [], []
    h = None
    modtabs = [_ada(cvec, w_ada[l], b_ada[l]).reshape(rows * N_MOD, 1, d_model) for l in range(depth)]
    h, = _ln_mod(geom, x, alpha, mod=(modtabs[0], 1, 0))
    for l in range(depth):
        modtab = modtabs[l]

        segs = [_matmul(h, w_in[l][:, offs[k]:offs[k + 1]].astype(BF16), seg_dtype[k])
                for k in range(len(sizes))]
        z, xbc, dt_raw, hq, hf, hi, hgate, rq, rk, rv, rgate, bgate = segs

        xc = _conv_silu(geom, xbc, m_conv_w[l], m_conv_b[l])
        dtt = dt_raw.reshape(geom.t_all // CHUNK, CHUNK, 2 * m_heads).transpose(0, 2, 1)
        bias = m_dt_bias[l].reshape(1, 2 * m_heads)
        neg_a = -jnp.exp(m_a_log[l].astype(F32)).reshape(1, 2 * m_heads)
        prm = (bias, neg_a, bias.T, neg_a.T, e_mat)
        s0 = state_ssm[:, l].reshape(bl, 2, M_GROUPS, hpg, m_state, m_headdim)
        s0 = s0.transpose(0, 1, 2, 4, 3, 5).reshape(bl, 2, M_GROUPS, m_state, gw)
        yf_c, yb_c, sfin = _ssd_scan(xc, dt_raw, dtt, prm, None, dims=ssd_dims, n_seq=bp, seq_len=seq, row_off=0)
        yf_l, yb_l = _ssd_scan(xc, dt_raw, dtt, prm, s0, dims=ssd_dims, n_seq=bl, seq_len=dec_seq, row_off=t_ctx)
        sfin = sfin.reshape(bp, 2, M_GROUPS, m_state, hpg, m_headdim).transpose(0, 1, 2, 4, 3, 5)
        out_ssm.append(sfin.reshape(bp, 2, m_heads, m_state, m_headdim))
        y_ssm = _group_rms_gate(geom, jnp.concatenate([yf_c, yf_l], 0), jnp.concatenate([yb_c, yb_l], 0),
                                z, 0, m_norm[l], gs=m_inner // M_GROUPS, gate_before=True,
                                skip=(xc, jnp.repeat(m_d[l], m_headdim)))

        lb = lower[l]
        prm = tuple(a.reshape(2, 1, hg_w) for a in (jnp.log(lb), jnp.log1p(-lb), 1.0 - lb))
        s0 = jnp.swapaxes(state_hgrn[:, l], -1, -2)
        of_c, ob_c, gfin = _gla_scan(hq, hi, hf, prm, None, heads=hg_heads, dk=hg_dk, dv=hg_dv,
                                     n_seq=bp, seq_len=seq, row_off=0)
        of_l, ob_l = _gla_scan(hq, hi, hf, prm, s0, heads=hg_heads, dk=hg_dk, dv=hg_dv,
                               n_seq=bl, seq_len=dec_seq, row_off=t_ctx)
        out_hg.append(jnp.swapaxes(gfin, -1, -2))
        y_hg = _group_rms_gate(geom, jnp.concatenate([of_c, of_l], 0), jnp.concatenate([ob_c, ob_l], 0),
                               hgate, 0, hg_norm[l], gs=hg_dk, gate_before=False)

        lg = jnp.broadcast_to(log_g[l][:, :, None, None], (2, ret_heads, 1, LANES))
        rf_c, rb_c, tfin = _ret_scan(rq, rk, rv, lg, None, None, heads=ret_heads, dk=ret_dk, dv=ret_dv,
                                     n_seq=bp, seq_len=seq, row_off=0)
        rf_l, rb_l = _ret_scan(rq, rk, rv, lg, rope, state_ret[:, l], heads=ret_heads, dk=ret_dk, dv=ret_dv,
                               n_seq=bl, seq_len=dec_seq, row_off=t_ctx)
        out_ret.append(tfin)
        y_ret = _group_rms_gate(geom, jnp.concatenate([rf_c, rf_l], 0), jnp.concatenate([rb_c, rb_l], 0),
                                rgate, 0, ret_norm[l], gs=ret_dk, gate_before=False)

        ps = _matmul(y_ssm, w_br_ssm[l].astype(BF16), F32)
        ph = _matmul(y_hg, w_br_hg[l].astype(BF16), F32)
        pr = _matmul(y_ret, w_br_ret[l].astype(BF16), F32)
        merged = _merge(geom, bgate, ps, ph, pr)
        mix = _matmul(merged, w_out[l].astype(BF16), F32)
        x, h = _ln_mod(geom, x, alpha, ln=(mix, modtab, 2, ln1_g[l], ln1_b[l]), mod=(modtab, 4, 3))

        q = _matmul(h, pk_query[l].astype(BF16), F32)
        route = _route(q, pk_keys[l].astype(BF16), topk=PK_TOPK)
        ff = _peer(h, peer_u[l].astype(BF16), peer_v[l].astype(BF16), route, heads=pk_heads, nk=n_keys)
        if l + 1 < depth:
            x, h = _ln_mod(geom, x, alpha, ln=(ff, modtab, 5, ln2_g[l], ln2_b[l]), mod=(modtabs[l + 1], 1, 0))
        else:
            x, = _ln_mod(geom, x, alpha, ln=(ff, modtab, 5, ln2_g[l], ln2_b[l]))

    dt_out = x_prompt.dtype
    y_p = x[:t_ctx].reshape(bp, seq, d_model)
    y_s = x[t_ctx:].reshape(bl, dec_seq, d_model)
    return (y_p, y_s, jnp.stack(out_ssm, axis=1).astype(dt_out), jnp.stack(out_hg, axis=1).astype(dt_out),
            jnp.stack(out_ret, axis=1).astype(dt_out))
```

```python
import functools
import math

import jax
import jax.numpy as jnp
from jax import lax
from jax.experimental import pallas as pl
from jax.experimental.pallas import tpu as pltpu

F32 = jnp.float32
BF16 = jnp.bfloat16

GRID_W = 64
M_GROUPS = 8
CHUNK = 64
HG_CHUNK = 32
PK_TOPK = 16
ROPE_BASE = 10000.0
EPS = 1e-6
N_MOD = 6

V7X_VMEM_BYTES = 64 * 2**20
VMEM_LIMIT = V7X_VMEM_BYTES - 8 * 2**20
LANES = 128
MXU_WIDTH = 256
MASKED = -1e30
GLA_HEADS_PER_STEP = 4


def _params(*sem):
    return pltpu.CompilerParams(dimension_semantics=sem, vmem_limit_bytes=VMEM_LIMIT)


def _silu(x):
    return x * jax.nn.sigmoid(x)


def _split3(a):
    hi = a.astype(BF16)
    r1 = a - hi.astype(F32)
    mid = r1.astype(BF16)
    lo = (r1 - mid.astype(F32)).astype(BF16)
    return hi, mid, lo


def _dot(a, b):
    return jnp.dot(a, b, preferred_element_type=F32)


def _dot_nt(a, b):
    return lax.dot_general(a, b, (((1,), (1,)), ((), ())), preferred_element_type=F32)


def _dot_tn(a, b):
    return lax.dot_general(a, b, (((0,), (0,)), ((), ())), preferred_element_type=F32)


def _exact_left(sel, a):
    hi, mid, lo = _split3(a)
    return _dot(sel, hi) + _dot(sel, mid) + _dot(sel, lo)


def _exact_right(a, sel):
    hi, mid, lo = _split3(a)
    return _dot(hi, sel) + _dot(mid, sel) + _dot(lo, sel)


class Geom:
    def __init__(self, n_ctx_seq, ctx_len, n_lat_seq, lat_len):
        self.n_ctx_seq, self.ctx_len, self.n_lat_seq, self.lat_len = n_ctx_seq, ctx_len, n_lat_seq, lat_len
        self.t_ctx = n_ctx_seq * ctx_len
        self.t_lat = n_lat_seq * lat_len
        self.t_all = self.t_ctx + self.t_lat
        self.rb = min(256, ctx_len)
        assert ctx_len % self.rb == 0 and lat_len % self.rb == 0
        self.ctx_blocks = self.t_ctx // self.rb
        self.ctx_bps = ctx_len // self.rb
        self.lat_bps = lat_len // self.rb

    def mod_row(self, i):
        return jnp.where(i < self.ctx_blocks, 0, 1 + (i - self.ctx_blocks) // self.lat_bps)


def _ada_kernel(c_ref, w_ref, b_ref, o_ref):
    a = _silu(c_ref[...]).astype(BF16)
    o_ref[...] = _dot(a, w_ref[...].astype(BF16)) + b_ref[...]


def _ada(cvec, w, b):
    r, d = cvec.shape
    n = w.shape[1]
    tn = _tile(n, 512)
    return pl.pallas_call(
        _ada_kernel, grid=(n // tn,),
        in_specs=[pl.BlockSpec((r, d), lambda j: (0, 0)),
                  pl.BlockSpec((d, tn), lambda j: (0, j)),
                  pl.BlockSpec((1, tn), lambda j: (0, j))],
        out_specs=pl.BlockSpec((r, tn), lambda j: (0, j)),
        out_shape=jax.ShapeDtypeStruct((r, n), F32),
        compiler_params=_params("parallel"), name="ada")(cvec, w, b.reshape(1, n))


def _ln_mod_kernel(*refs, alpha, has_ln, has_mod):
    refs = list(refs)
    x_ref = refs.pop(0)
    x = x_ref[...]
    if has_ln:
        mix_ref, gate_ref, g_ref, b_ref = refs[:4]
        refs = refs[4:]
    if has_mod:
        sc_ref, sh_ref = refs[:2]
        refs = refs[2:]
    if has_ln:
        y = alpha * x + gate_ref[0] * mix_ref[...]
        mu = jnp.mean(y, axis=-1, keepdims=True)
        yc = y - mu
        var = jnp.mean(yc * yc, axis=-1, keepdims=True)
        x = yc * lax.rsqrt(var + EPS) * g_ref[...] + b_ref[...]
        xo_ref = refs.pop(0)
        xo_ref[...] = x
    if has_mod:
        ho_ref = refs.pop(0)
        ho_ref[...] = (x * (1.0 + sc_ref[0]) + sh_ref[0]).astype(BF16)


def _ln_mod(geom, x, alpha, ln=None, mod=None):
    t, d = x.shape
    rb = geom.rb
    row = pl.BlockSpec((rb, d), lambda i: (i, 0))
    vec = pl.BlockSpec((1, d), lambda i: (0, 0))

    def slot(k):
        return pl.BlockSpec((1, 1, d), lambda i: (geom.mod_row(i) * N_MOD + k, 0, 0))

    args, specs, outs, ospecs = [x], [row], [], []
    if ln is not None:
        mix, tab, gk, g, b = ln
        args += [mix, tab, g.reshape(1, d), b.reshape(1, d)]
        specs += [row, slot(gk), vec, vec]
        outs.append(jax.ShapeDtypeStruct((t, d), F32))
        ospecs.append(row)
    if mod is not None:
        tab, sk, hk = mod
        args += [tab, tab]
        specs += [slot(sk), slot(hk)]
        outs.append(jax.ShapeDtypeStruct((t, d), BF16))
        ospecs.append(row)
    res = pl.pallas_call(
        functools.partial(_ln_mod_kernel, alpha=alpha, has_ln=ln is not None, has_mod=mod is not None),
        grid=(t // rb,), in_specs=specs, out_specs=ospecs, out_shape=outs,
        compiler_params=_params("parallel"), name="ln_mod")(*args)
    return res


def _mm_kernel(x_ref, w_ref, o_ref):
    o_ref[...] = _dot(x_ref[...], w_ref[...]).astype(o_ref.dtype)


def _tile(n, want):
    if n <= want:
        return n
    t = want
    while n % t:
        t //= 2
    return t


def _matmul(x, w, out_dtype, tm=1024, tn=512):
    m, k = x.shape
    n = w.shape[1]
    tm, tn = _tile(m, tm), _tile(n, tn)
    assert tm % 8 == 0 and tn % LANES == 0
    return pl.pallas_call(
        _mm_kernel, grid=(m // tm, n // tn),
        in_specs=[pl.BlockSpec((tm, k), lambda i, j: (i, 0)),
                  pl.BlockSpec((k, tn), lambda i, j: (0, j))],
        out_specs=pl.BlockSpec((tm, tn), lambda i, j: (i, j)),
        out_shape=jax.ShapeDtypeStruct((m, n), out_dtype),
        compiler_params=_params("parallel", "arbitrary"), name="matmul")(x, w)


CONV_HALO = 16


def _conv_kernel(prev_ref, cur_ref, next_ref, w_ref, b_ref, o_ref, *, geom, conv_w):
    i = pl.program_id(0)
    rb = geom.rb
    in_ctx = i < geom.ctx_blocks
    k_ctx = i % geom.ctx_bps
    k_lat = (i - geom.ctx_blocks) % geom.lat_bps
    is_start = jnp.where(in_ctx, k_ctx == 0, k_lat == 0)
    is_end = jnp.where(in_ctx, k_ctx == geom.ctx_bps - 1, k_lat == geom.lat_bps - 1)
    prev = jnp.where(is_start, 0.0, prev_ref[...].astype(F32))
    nxt = jnp.where(is_end, 0.0, next_ref[...].astype(F32))
    z = jnp.concatenate([prev, cur_ref[...].astype(F32), nxt], axis=0)
    rows = rb + 2 * CONV_HALO
    pad = conv_w // 2
    acc = jnp.zeros(o_ref.shape, F32) + b_ref[...]
    for k in range(conv_w):
        zk = z if k == pad else pltpu.roll(z, (pad - k) % rows, 0)
        acc = acc + w_ref[k:k + 1, :] * zk[CONV_HALO:CONV_HALO + rb]
    o_ref[...] = _silu(acc).astype(o_ref.dtype)


def _conv_silu(geom, x, w, b):
    t, c = x.shape
    rb = geom.rb
    cb = _tile(c, 512)
    assert rb % CONV_HALO == 0
    hb = rb // CONV_HALO
    last = t // CONV_HALO - 1
    conv_w = w.shape[0]
    return pl.pallas_call(
        functools.partial(_conv_kernel, geom=geom, conv_w=conv_w), grid=(t // rb, c // cb),
        in_specs=[pl.BlockSpec((CONV_HALO, cb), lambda i, j: (jnp.maximum(i * hb - 1, 0), j)),
                  pl.BlockSpec((rb, cb), lambda i, j: (i, j)),
                  pl.BlockSpec((CONV_HALO, cb), lambda i, j: (jnp.minimum((i + 1) * hb, last), j)),
                  pl.BlockSpec((conv_w, cb), lambda i, j: (0, j)),
                  pl.BlockSpec((1, cb), lambda i, j: (0, j))],
        out_specs=pl.BlockSpec((rb, cb), lambda i, j: (i, j)),
        out_shape=jax.ShapeDtypeStruct((t, c), BF16),
        compiler_params=_params("parallel", "parallel"), name="conv_silu")(x, x, x, w, b.reshape(1, c))


def _ssd_kernel(*refs, dims, has_init):
    h_all, g_all, n_state, p_dim = dims
    hpg = h_all // g_all
    inner = h_all * p_dim
    gw = hpg * p_dim
    L = CHUNK
    refs = list(refs)
    xc = refs[0:2]
    dt = refs[2:4]
    dtt = refs[4:6]
    brow, arow, bcol, acol, e_ref = refs[6:11]
    refs = refs[11:]
    if has_init:
        s0_ref = refs.pop(0)
    y_out = refs[0:2]
    refs = refs[2:]
    if not has_init:
        sfin_ref = refs.pop(0)
    s_scr, cumx_s, ecx_s, xdt_s, xs_s, cumt_s, elast_s = refs

    c = pl.program_id(1)

    @pl.when(c == 0)
    def _():
        if has_init:
            s_scr[...] = s0_ref[0]
        else:
            s_scr[...] = jnp.zeros(s_scr.shape, F32)

    ri = lax.broadcasted_iota(jnp.int32, (L, L), 0)
    ci = lax.broadcasted_iota(jnp.int32, (L, L), 1)
    e_mat = e_ref[...]

    for d in range(2):
        lo, hi = d * h_all, (d + 1) * h_all
        keep = (ci <= ri) if d == 0 else (ci >= ri)
        tri = keep.astype(BF16)
        dtv = jax.nn.softplus(dt[d][:, lo:hi] + brow[:, lo:hi])
        cum = _exact_left(tri, dtv * arow[:, lo:hi])
        cumx = _exact_right(cum, e_mat)
        dtx = _exact_right(dtv, e_mat)
        last = cumx[L - 1:L, :] if d == 0 else cumx[0:1, :]
        xdt = xc[d][:, :inner].astype(F32) * dtx
        cumx_s[...] = cumx
        ecx_s[...] = jnp.exp(cumx)
        xdt_s[...] = xdt.astype(BF16)
        xs_s[...] = (xdt * jnp.exp(last - cumx)).astype(BF16)
        elast_s[...] = jnp.broadcast_to(jnp.exp(last), elast_s.shape)
        lat = jax.nn.softplus(dtt[d][0, lo:hi, :] + bcol[lo:hi, :]) * acol[lo:hi, :]
        keep_t = (ri <= ci) if d == 0 else (ri >= ci)
        cumt_s[...] = _exact_right(lat, keep_t.astype(BF16))

        def group(g, carry, d=d, keep=keep):
            off_b = pl.multiple_of(inner + g * n_state, n_state)
            off_c = pl.multiple_of(inner + (g_all + g) * n_state, n_state)
            off_g = pl.multiple_of(g * gw, gw)
            bg = xc[d][:, pl.ds(off_b, n_state)]
            cg = xc[d][:, pl.ds(off_c, n_state)]
            scores = _dot_nt(cg, bg)
            s_old = s_scr[d, g]
            inter = _dot(cg, s_old.astype(BF16)) * ecx_s[:, pl.ds(off_g, gw)]
            cum_g = cumx_s[:, pl.ds(off_g, gw)]
            xdt_g = xdt_s[:, pl.ds(off_g, gw)]
            ys = []
            for h in range(hpg):
                col = cum_g[:, h * p_dim:h * p_dim + 1]
                row = cumt_s[pl.ds(g * hpg + h, 1), :]
                dec = jnp.exp(jnp.where(keep, col - row, -jnp.inf))
                ys.append(_dot((scores * dec).astype(BF16), xdt_g[:, h * p_dim:(h + 1) * p_dim]))
            y_out[d][:, pl.ds(off_g, gw)] = jnp.concatenate(ys, axis=1) + inter
            s_scr[d, g] = s_old * elast_s[0:1, pl.ds(off_g, gw)] + _dot_tn(bg, xs_s[:, pl.ds(off_g, gw)])
            return carry

        lax.fori_loop(0, g_all, group, 0)

    if not has_init:
        @pl.when(c == pl.num_programs(1) - 1)
        def _():
            sfin_ref[0] = s_scr[...]


def _ssd_scan(xc, dt, dtt, prm, s0, *, dims, n_seq, seq_len, row_off):
    h_all, g_all, n_state, p_dim = dims
    inner, gw = h_all * p_dim, (h_all // g_all) * p_dim
    L = CHUNK
    nc = seq_len // L
    off = row_off // L
    cw = xc.shape[1]
    has_init = s0 is not None
    brow, arow, bcol, acol, e_mat = prm

    def fwd(b, c):
        return off + b * nc + c

    def bwd(b, c):
        return off + b * nc + (nc - 1 - c)

    full = lambda a: pl.BlockSpec(a.shape, lambda b, c: (0,) * a.ndim)
    in_specs = [pl.BlockSpec((L, cw), lambda b, c: (fwd(b, c), 0)),
                pl.BlockSpec((L, cw), lambda b, c: (bwd(b, c), 0)),
                pl.BlockSpec((L, 2 * h_all), lambda b, c: (fwd(b, c), 0)),
                pl.BlockSpec((L, 2 * h_all), lambda b, c: (bwd(b, c), 0)),
                pl.BlockSpec((1, 2 * h_all, L), lambda b, c: (fwd(b, c), 0, 0)),
                pl.BlockSpec((1, 2 * h_all, L), lambda b, c: (bwd(b, c), 0, 0)),
                full(brow), full(arow), full(bcol), full(acol), full(e_mat)]
    args = [xc, xc, dt, dt, dtt, dtt, brow, arow, bcol, acol, e_mat]
    st_block = (1, 2, g_all, n_state, gw)
    if has_init:
        in_specs.append(pl.BlockSpec(st_block, lambda b, c: (b, 0, 0, 0, 0)))
        args.append(s0)
    t_loc = n_seq * seq_len
    out_shape = [jax.ShapeDtypeStruct((t_loc, inner), F32)] * 2
    out_specs = [pl.BlockSpec((L, inner), lambda b, c: (b * nc + c, 0)),
                 pl.BlockSpec((L, inner), lambda b, c: (b * nc + (nc - 1 - c), 0))]
    if not has_init:
        out_shape.append(jax.ShapeDtypeStruct((n_seq,) + st_block[1:], F32))
        out_specs.append(pl.BlockSpec(st_block, lambda b, c: (b, 0, 0, 0, 0)))
    scratch = [pltpu.VMEM((2, g_all, n_state, gw), F32),
               pltpu.VMEM((L, inner), F32), pltpu.VMEM((L, inner), F32),
               pltpu.VMEM((L, inner), BF16), pltpu.VMEM((L, inner), BF16),
               pltpu.VMEM((h_all, L), F32), pltpu.VMEM((8, inner), F32)]
    return pl.pallas_call(
        functools.partial(_ssd_kernel, dims=dims, has_init=has_init), grid=(n_seq, nc),
        in_specs=in_specs, out_specs=out_specs, out_shape=out_shape, scratch_shapes=scratch,
        compiler_params=_params("parallel", "arbitrary"), name="ssd_scan")(*args)


def _gla_kernel(*refs, tb, dk, dv, hps, has_init):
    L = HG_CHUNK
    nch = tb // L
    mid = L // 2
    refs = list(refs)
    q_r, v_r, f_r = refs[0:2], refs[2:4], refs[4:6]
    loglb, log1m, onem = refs[6:8], refs[8:10], refs[10:12]
    refs = refs[12:]
    if has_init:
        s0_ref = refs.pop(0)
    o_r = refs[0:2]
    refs = refs[2:]
    if not has_init:
        sfin_ref = refs.pop(0)
    st_scr, = refs

    tblk = pl.program_id(2)

    @pl.when(tblk == 0)
    def _():
        if has_init:
            st_scr[...] = s0_ref[0]
        else:
            st_scr[...] = jnp.zeros(st_scr.shape, F32)

    ri = lax.broadcasted_iota(jnp.int32, (tb, tb), 0)
    ci = lax.broadcasted_iota(jnp.int32, (tb, tb), 1)
    same = (ri // L) == (ci // L)
    rl = lax.broadcasted_iota(jnp.int32, (L, L), 0)
    cl = lax.broadcasted_iota(jnp.int32, (L, L), 1)

    for d in range(2):
        hf = f_r[d][...]
        log_sig = jnp.minimum(hf, 0.0) - jnp.log1p(jnp.exp(-jnp.abs(hf)))
        a = loglb[d][0]
        b = log1m[d][0] + log_sig
        log_f = jnp.maximum(a, b) + jnp.log1p(jnp.exp(-jnp.abs(a - b)))
        k = onem[d][0] * jax.nn.sigmoid(-hf)
        q = _silu(q_r[d][...].astype(F32))
        v = v_r[d][...]
        tri_keep = same & ((ci <= ri) if d == 0 else (ci >= ri))
        cum = _exact_left(tri_keep.astype(BF16), log_f)
        ref_row = mid if d == 0 else mid - 1
        last_row = L - 1 if d == 0 else 0
        refx = jnp.concatenate([jnp.broadcast_to(cum[j * L + ref_row:j * L + ref_row + 1], (L, cum.shape[1]))
                                for j in range(nch)], axis=0)
        lastx = jnp.concatenate([jnp.broadcast_to(cum[j * L + last_row:j * L + last_row + 1], (L, cum.shape[1]))
                                 for j in range(nch)], axis=0)
        qd = (q * jnp.exp(cum - refx)).astype(BF16)
        kd = (k * jnp.exp(refx - cum)).astype(BF16)
        qs = (q * jnp.exp(cum)).astype(BF16)
        ks = (k * jnp.exp(lastx - cum)).astype(BF16)
        elast = jnp.exp(lastx)
        keep = (cl <= rl) if d == 0 else (cl >= rl)
        order = range(nch) if d == 0 else range(nch - 1, -1, -1)
        for hh in range(hps):
            kc = slice(hh * dk, (hh + 1) * dk)
            vc = slice(hh * dv, (hh + 1) * dv)
            st = st_scr[d, hh]
            for j in order:
                sl = slice(j * L, (j + 1) * L)
                sc = jnp.where(keep, _dot_nt(qd[sl, kc], kd[sl, kc]), 0.0)
                y = _dot(sc.astype(BF16), v[sl, vc]) + _dot_nt(qs[sl, kc], st.astype(BF16))
                o_r[d][sl, vc] = y
                st = st * elast[j * L:j * L + 1, kc] + _dot_tn(v[sl, vc], ks[sl, kc])
            st_scr[d, hh] = st

    if not has_init:
        @pl.when(tblk == pl.num_programs(2) - 1)
        def _():
            sfin_ref[0] = st_scr[...]


def _gla_scan(hq, hi, hf, prm, s0, *, heads, dk, dv, n_seq, seq_len, row_off):
    tb = min(256, seq_len)
    ntb = seq_len // tb
    off = row_off // tb
    has_init = s0 is not None
    loglb, log1m, onem = prm
    hps = GLA_HEADS_PER_STEP if heads % GLA_HEADS_PER_STEP == 0 else 1
    hblk = heads // hps

    def fwd(b, h, t):
        return off + b * ntb + t

    def bwd(b, h, t):
        return off + b * ntb + (ntb - 1 - t)

    def tok(rowf, col0):
        return pl.BlockSpec((tb, hps * dk), lambda b, h, t: (rowf(b, h, t), col0 + h))

    def par(d):
        return pl.BlockSpec((1, 1, hps * dk), lambda b, h, t: (d, 0, h))

    in_specs = [tok(fwd, 0), tok(bwd, 0), tok(fwd, 0), tok(bwd, 0), tok(fwd, 0), tok(bwd, hblk),
                par(0), par(1), par(0), par(1), par(0), par(1)]
    args = [hq, hq, hi, hi, hf, hf, loglb, loglb, log1m, log1m, onem, onem]
    st_block = (1, 2, hps, dv, dk)
    st_map = lambda b, h, t: (b, 0, h, 0, 0)
    if has_init:
        in_specs.append(pl.BlockSpec(st_block, st_map))
        args.append(s0)
    t_loc = n_seq * seq_len
    out_shape = [jax.ShapeDtypeStruct((t_loc, heads * dv), F32)] * 2
    out_specs = [pl.BlockSpec((tb, hps * dv), lambda b, h, t: (b * ntb + t, h)),
                 pl.BlockSpec((tb, hps * dv), lambda b, h, t: (b * ntb + (ntb - 1 - t), h))]
    if not has_init:
        out_shape.append(jax.ShapeDtypeStruct((n_seq, 2, heads, dv, dk), F32))
        out_specs.append(pl.BlockSpec(st_block, st_map))
    return pl.pallas_call(
        functools.partial(_gla_kernel, tb=tb, dk=dk, dv=dv, hps=hps, has_init=has_init), grid=(n_seq, hblk, ntb),
        in_specs=in_specs, out_specs=out_specs, out_shape=out_shape,
        scratch_shapes=[pltpu.VMEM((2, hps, dv, dk), F32)],
        compiler_params=_params("parallel", "parallel", "arbitrary"), name="gla_scan")(*args)


def _ret_kernel(*refs, tb, dk, has_init, scale):
    L = CHUNK
    nch = tb // L
    refs = list(refs)
    q_r, k_r, v_r = refs[0:2], refs[2:4], refs[4:6]
    lg_r = refs[6:8]
    refs = refs[8:]
    if has_init:
        cos_r, sin_r = refs[0:2], refs[2:4]
        s0_ref = refs[4]
        refs = refs[5:]
    o_r = refs[0:2]
    refs = refs[2:]
    if not has_init:
        sfin_ref = refs.pop(0)
    s_scr, = refs

    tblk = pl.program_id(2)

    @pl.when(tblk == 0)
    def _():
        if has_init:
            s_scr[...] = s0_ref[0, :, 0]
        else:
            s_scr[...] = jnp.zeros(s_scr.shape, F32)

    rl = lax.broadcasted_iota(jnp.int32, (L, L), 0)
    cl = lax.broadcasted_iota(jnp.int32, (L, L), 1)
    pos = lax.broadcasted_iota(jnp.int32, (L, dk), 0).astype(F32)

    def rope(x, cos, sin):
        parts = [pltpu.roll(x[:, i:i + LANES], LANES // 2, 1) for i in range(0, dk, LANES)]
        return x * cos + jnp.concatenate(parts, axis=1) * sin

    for d in range(2):
        lg = lg_r[d][0, 0]
        lgk = jnp.concatenate([lg] * (dk // LANES), axis=1)
        q = q_r[d][...].astype(F32)
        k = k_r[d][...].astype(F32) * scale
        if has_init:
            q = rope(q, cos_r[d][...], sin_r[d][...])
            k = rope(k, cos_r[d][...], sin_r[d][...])
        q = q.astype(BF16)
        k = k.astype(BF16)
        v = v_r[d][...]
        if d == 0:
            keep = cl <= rl
            dmat = jnp.exp(jnp.where(keep, (rl - cl).astype(F32) * lg[:, :L], -jnp.inf))
            w_q = jnp.exp((pos + 1.0) * lgk)
            w_k = jnp.exp((L - 1.0 - pos) * lgk)
        else:
            keep = cl >= rl
            dmat = jnp.exp(jnp.where(keep, (cl - rl).astype(F32) * lg[:, :L], -jnp.inf))
            w_q = jnp.exp((L - pos) * lgk)
            w_k = jnp.exp(pos * lgk)
        e_chunk = jnp.exp(float(L) * lgk)
        s = s_scr[d]
        order = range(nch) if d == 0 else range(nch - 1, -1, -1)
        for j in order:
            sl = slice(j * L, (j + 1) * L)
            qj, kj, vj = q[sl], k[sl], v[sl]
            sc = _dot_nt(qj, kj) * dmat
            y = _dot(sc.astype(BF16), vj) + _dot(qj, s.astype(BF16)) * w_q[:, :v.shape[1]]
            o_r[d][sl, :] = y
            kw = (kj.astype(F32) * w_k).astype(BF16)
            s = s * e_chunk[:, :1] + _dot_tn(kw, vj)
        s_scr[d] = s

    if not has_init:
        @pl.when(tblk == pl.num_programs(2) - 1)
        def _():
            sfin_ref[0, :, 0] = s_scr[...]


def _ret_scan(rq, rk, rv, lg, rope, s0, *, heads, dk, dv, n_seq, seq_len, row_off):
    assert dk == dv
    tb = min(256, seq_len)
    ntb = seq_len // tb
    off = row_off // tb
    has_init = s0 is not None

    def fwd(b, h, t):
        return off + b * ntb + t

    def bwd(b, h, t):
        return off + b * ntb + (ntb - 1 - t)

    def tok(rowf):
        return pl.BlockSpec((tb, dk), lambda b, h, t: (rowf(b, h, t), h))

    def lgs(d):
        return pl.BlockSpec((1, 1, 1, LANES), lambda b, h, t: (d, h, 0, 0))

    in_specs = [tok(fwd), tok(bwd), tok(fwd), tok(bwd), tok(fwd), tok(bwd), lgs(0), lgs(1)]
    args = [rq, rq, rk, rk, rv, rv, lg, lg]
    st_block = (1, 2, 1, dk, dv)
    st_map = lambda b, h, t: (b, 0, h, 0, 0)
    if has_init:
        cos, sin = rope
        pf = pl.BlockSpec((tb, dk), lambda b, h, t: (t, 0))
        pb = pl.BlockSpec((tb, dk), lambda b, h, t: (ntb - 1 - t, 0))
        in_specs += [pf, pb, pf, pb, pl.BlockSpec(st_block, st_map)]
        args += [cos, cos, sin, sin, s0]
    t_loc = n_seq * seq_len
    out_shape = [jax.ShapeDtypeStruct((t_loc, heads * dv), F32)] * 2
    out_specs = [pl.BlockSpec((tb, dv), lambda b, h, t: (b * ntb + t, h)),
                 pl.BlockSpec((tb, dv), lambda b, h, t: (b * ntb + (ntb - 1 - t), h))]
    if not has_init:
        out_shape.append(jax.ShapeDtypeStruct((n_seq, 2, heads, dk, dv), F32))
        out_specs.append(pl.BlockSpec(st_block, st_map))
    return pl.pallas_call(
        functools.partial(_ret_kernel, tb=tb, dk=dk, has_init=has_init, scale=dk ** -0.5),
        grid=(n_seq, heads, ntb),
        in_specs=in_specs, out_specs=out_specs, out_shape=out_shape,
        scratch_shapes=[pltpu.VMEM((2, dk, dv), F32)],
        compiler_params=_params("parallel", "parallel", "arbitrary"), name="ret_scan")(*args)


def _grms_kernel(*refs, gs, gate_before, has_skip, ctx_blocks):
    refs = list(refs)
    ac_ref, bc_ref, al_ref, bl_ref, gate_ref, w_ref = refs[:6]
    refs = refs[6:]
    if has_skip:
        x_ref, d_ref = refs[:2]
        refs = refs[2:]
    o_ref, = refs
    y = jnp.where(pl.program_id(0) < ctx_blocks, ac_ref[...] + bc_ref[...], al_ref[...] + bl_ref[...])
    if has_skip:
        y = y + d_ref[...] * x_ref[...].astype(F32)
    gate = _silu(gate_ref[...].astype(F32))
    if gate_before:
        y = y * gate
    width = y.shape[1]
    outs = []
    for g in range(width // gs):
        yg = y[:, g * gs:(g + 1) * gs]
        ms = jnp.mean(yg * yg, axis=-1, keepdims=True)
        outs.append(yg * lax.rsqrt(ms + EPS))
    yn = jnp.concatenate(outs, axis=1) * w_ref[...]
    if not gate_before:
        yn = yn * gate
    o_ref[...] = yn.astype(o_ref.dtype)


def _group_rms_gate(geom, ab_ctx, ab_lat, gate, gate_col0, w, *, gs, gate_before, skip=None):
    t = geom.t_all
    width = ab_ctx[0].shape[1]
    rb = geom.rb
    cb = max(_tile(width, 512), gs)
    assert width % cb == 0 and cb % gs == 0 and gate_col0 % cb == 0
    nctx, ncb = geom.ctx_blocks, width // cb
    row = pl.BlockSpec((rb, cb), lambda i, j: (i, j))
    vec = pl.BlockSpec((1, cb), lambda i, j: (0, j))
    ctx = pl.BlockSpec((rb, cb), lambda i, j: (jnp.minimum(i, nctx - 1), jnp.where(i < nctx, j, ncb - 1)))
    lat = pl.BlockSpec((rb, cb), lambda i, j: (jnp.maximum(i - nctx, 0), jnp.where(i < nctx, 0, j)))
    g0 = gate_col0 // cb
    in_specs = [ctx, ctx, lat, lat, pl.BlockSpec((rb, cb), lambda i, j: (i, g0 + j)), vec]
    args = [*ab_ctx, *ab_lat, gate, w.reshape(1, width)]
    if skip is not None:
        x, dvec = skip
        in_specs += [row, vec]
        args += [x, dvec.reshape(1, width)]
    return pl.pallas_call(
        functools.partial(_grms_kernel, gs=gs, gate_before=gate_before, has_skip=skip is not None, ctx_blocks=nctx),
        grid=(t // rb, width // cb), in_specs=in_specs, out_specs=row,
        out_shape=jax.ShapeDtypeStruct((t, width), BF16),
        compiler_params=_params("parallel", "parallel"), name="group_rms_gate")(*args)


def _merge_kernel(gs_ref, gh_ref, gr_ref, ps_ref, ph_ref, pr_ref, o_ref):
    acc = jax.nn.sigmoid(gs_ref[...].astype(F32)) * ps_ref[...]
    acc = acc + jax.nn.sigmoid(gh_ref[...].astype(F32)) * ph_ref[...]
    acc = acc + jax.nn.sigmoid(gr_ref[...].astype(F32)) * pr_ref[...]
    o_ref[...] = acc.astype(o_ref.dtype)


def _merge(geom, bgate, ps, ph, pr):
    t, d = ps.shape
    rb = geom.rb
    cb = _tile(d, 1024)
    nb = d // cb
    row = pl.BlockSpec((rb, cb), lambda i, j: (i, j))
    gate = lambda k: pl.BlockSpec((rb, cb), lambda i, j: (i, k * nb + j))
    return pl.pallas_call(
        _merge_kernel, grid=(t // rb, nb),
        in_specs=[gate(0), gate(1), gate(2), row, row, row],
        out_specs=row, out_shape=jax.ShapeDtypeStruct((t, d), BF16),
        compiler_params=_params("parallel", "parallel"), name="merge")(bgate, bgate, bgate, ps, ph, pr)


def _topk_desc(s, k):
    rows = []
    cur = s
    for _ in range(k):
        m = jnp.max(cur, axis=0, keepdims=True)
        rows.append(m)
        cur = jnp.where(cur == m, -jnp.inf, cur)
    return rows


def _route_kernel(q_ref, keys_ref, a1_ref, s1_ref, a2_ref, s2_ref, tau_ref, *, heads, nk, half, topk):
    tm = q_ref.shape[0]
    row_k = lax.broadcasted_iota(jnp.int32, (topk, tm), 0)
    row_t = lax.broadcasted_iota(jnp.int32, (tau_ref.shape[0], tm), 0)
    tau_all = jnp.zeros(tau_ref.shape, F32)
    for h in range(heads):
        tops, masked = [], []
        for c in range(2):
            qh = q_ref[:, (2 * h + c) * half:(2 * h + c + 1) * half].astype(BF16)
            s = _dot_nt(keys_ref[h, c], qh)
            rows = _topk_desc(s, topk)
            tops.append(rows)
            masked.append(jnp.where(s >= rows[-1], s, MASKED))
        v2 = jnp.zeros((topk, tm), F32)
        for b in range(topk):
            v2 = jnp.where(row_k == b, tops[1][b], v2)
        cand = jnp.concatenate([tops[0][a] + v2 for a in range(topk)], axis=0)
        best = _topk_desc(cand, topk)
        tau = best[-1]
        z = sum(jnp.exp(b - best[0]) for b in best)
        inv_z = 1.0 / z
        a1_ref[h * nk:(h + 1) * nk, :] = jnp.exp(masked[0] - tops[0][0]) * inv_z
        a2_ref[h * nk:(h + 1) * nk, :] = jnp.exp(masked[1] - tops[1][0])
        s1_ref[h * nk:(h + 1) * nk, :] = masked[0]
        s2_ref[h * nk:(h + 1) * nk, :] = masked[1]
        tau_all = jnp.where(row_t == h, tau, tau_all)
    tau_ref[...] = tau_all


def _route(q, keys, *, topk):
    t = q.shape[0]
    heads, _, nk, half = keys.shape
    tm = _tile(t, 256)
    assert tm % LANES == 0
    tau_rows = 8 * ((heads + 7) // 8)
    col = lambda r: pl.BlockSpec((r, tm), lambda i: (0, i))
    wide = jax.ShapeDtypeStruct((heads * nk, t), F32)
    return pl.pallas_call(
        functools.partial(_route_kernel, heads=heads, nk=nk, half=half, topk=topk), grid=(t // tm,),
        in_specs=[pl.BlockSpec((tm, q.shape[1]), lambda i: (i, 0)), pl.BlockSpec(keys.shape, lambda i: (0, 0, 0, 0))],
        out_specs=[col(heads * nk)] * 4 + [col(tau_rows)],
        out_shape=[wide] * 4 + [jax.ShapeDtypeStruct((tau_rows, t), F32)],
        compiler_params=_params("parallel"), name="peer_route")(q, keys)


def _peer_kernel(x_ref, u_ref, v_ref, a1_ref, s1_ref, a2_ref, s2_ref, tau_ref, o_ref, *, heads, nk):
    j = pl.program_id(1)
    tn = u_ref.shape[0]
    nsub = tn // nk
    x = x_ref[...]
    gates = []
    for e in range(nsub):
        e1 = j * nsub + e
        gate_t = jnp.zeros((nk, x.shape[0]), F32)
        for h in range(heads):
            s1 = s1_ref[pl.ds(h * nk + e1, 1), :]
            a1 = a1_ref[pl.ds(h * nk + e1, 1), :]
            hit = (s1 + s2_ref[h * nk:(h + 1) * nk, :]) >= tau_ref[h:h + 1, :]
            gate_t = gate_t + jnp.where(hit, a1 * a2_ref[h * nk:(h + 1) * nk, :], 0.0)
        gates.append(gate_t.T)
    blocks = []
    wide = min(tn, MXU_WIDTH)
    for p in range(tn // wide):
        pre = _dot_nt(x, u_ref[p * wide:(p + 1) * wide, :])
        act = 0.5 * pre * (1.0 + lax.erf(pre * (1.0 / math.sqrt(2.0))))
        for q in range(wide // nk):
            blocks.append((act[:, q * nk:(q + 1) * nk] * gates[p * (wide // nk) + q]).astype(BF16))
    contrib = _dot(jnp.concatenate(blocks, axis=1), v_ref[...])

    @pl.when(j == 0)
    def _():
        o_ref[...] = contrib

    @pl.when(j > 0)
    def _():
        o_ref[...] += contrib


def _peer(x, u, v, route, *, heads, nk, tm=512, tn=512):
    t, d = x.shape
    e_all = u.shape[0]
    tm, tn = _tile(t, tm), _tile(e_all, tn)
    assert tn % nk == 0 and tm % LANES == 0
    a1, s1, a2, s2, tau = route
    once = dict(pipeline_mode=pl.Buffered(1))
    rowx = pl.BlockSpec((tm, d), lambda i, j: (i, 0), **once)
    tab = pl.BlockSpec((tn, d), lambda i, j: (j, 0))
    wide = pl.BlockSpec((heads * nk, tm), lambda i, j: (0, i), **once)
    taus = pl.BlockSpec((tau.shape[0], tm), lambda i, j: (0, i), **once)
    return pl.pallas_call(
        functools.partial(_peer_kernel, heads=heads, nk=nk), grid=(t // tm, e_all // tn),
        in_specs=[rowx, tab, tab, wide, wide, wide, wide, taus],
        out_specs=pl.BlockSpec((tm, d), lambda i, j: (i, 0)), out_shape=jax.ShapeDtypeStruct((t, d), F32),
        compiler_params=_params("parallel", "arbitrary"), name="peer_dense")(x, u, v, a1, s1, a2, s2, tau)


def _rope_tables(seq_len, dk):
    half, quarter = dk // 2, dk // 4
    freqs = ROPE_BASE ** (-jnp.arange(quarter, dtype=F32) / quarter)
    t = jnp.arange(seq_len)
    ang_r = (t // GRID_W).astype(F32)[:, None] * freqs
    ang_c = (t % GRID_W).astype(F32)[:, None] * freqs
    cos = jnp.concatenate([jnp.cos(ang_r)] * 2 + [jnp.cos(ang_c)] * 2, axis=1)
    sin = jnp.concatenate([-jnp.sin(ang_r), jnp.sin(ang_r), -jnp.sin(ang_c), jnp.sin(ang_c)], axis=1)
    return cos, sin


def kernel(x_prompt, x_sample, c, state_ssm, state_hgrn, state_ret, c_ctx, w_ada, b_ada, w_in, m_conv_w, m_conv_b, m_dt_bias, m_a_log, m_d, m_norm, hg_lower_bounds, hg_norm, ret_decay, ret_norm, w_br_ssm, w_br_hg, w_br_ret, w_out, ln1_g, ln1_b, ln2_g, ln2_b, pk_query, pk_keys, peer_u, peer_v):
    bp, seq, d_model = x_prompt.shape
    bl, dec_seq, _ = x_sample.shape
    depth = w_in.shape[0]
    m_heads, m_state, m_headdim = state_ssm.shape[3:]
    hg_heads, hg_dk, hg_dv = state_hgrn.shape[3:]
    ret_heads, ret_dk, ret_dv = state_ret.shape[3:]
    m_inner = m_heads * m_headdim
    m_bc = M_GROUPS * m_state
    hg_w = hg_heads * hg_dk
    ret_w = ret_heads * ret_dk
    pk_heads, _, n_keys, pk_half = pk_keys.shape[1:]
    hpg = m_heads // M_GROUPS
    gw = hpg * m_headdim
    alpha = (2 * depth) ** 0.25
    geom = Geom(bp, seq, bl, dec_seq)
    t_ctx = geom.t_ctx
    ssd_dims = (m_heads, M_GROUPS, m_state, m_headdim)

    sizes = (m_inner, m_inner + 2 * m_bc, 2 * m_heads, hg_w, 2 * hg_w, hg_w, hg_w, ret_w, ret_w, ret_w, ret_w, 3 * d_model)
    offs = [0]
    for s in sizes:
        offs.append(offs[-1] + s)
    seg_dtype = (BF16, BF16, F32, BF16, F32, BF16, BF16, BF16, BF16, BF16, BF16, BF16)

    x = jnp.concatenate([x_prompt.reshape(t_ctx, d_model), x_sample.reshape(geom.t_lat, d_model)], axis=0)
    rows = 8 * ((1 + bl + 7) // 8)
    cvec = jnp.zeros((rows, d_model), F32).at[0].set(c_ctx).at[1:1 + bl].set(c)

    p_lb = jax.nn.softmax(hg_lower_bounds.astype(F32), axis=0)
    lower = jnp.cumsum(p_lb, axis=0) - p_lb[:1]
    rope = _rope_tables(dec_seq, ret_dk)
    e_mat = jnp.repeat(jnp.eye(m_heads, dtype=BF16), m_headdim, axis=1)
    log_g = jax.nn.log_sigmoid(ret_decay.astype(F32))

    out_ssm, out_hg, out_ret = [], [], []
    h = None
    modtabs = [_ada(cvec, w_ada[l], b_ada[l]).reshape(rows * N_MOD, 1, d_model) for l in range(depth)]
    h, = _ln_mod(geom, x, alpha, mod=(modtabs[0], 1, 0))
    for l in range(depth):
        modtab = modtabs[l]

        segs = [_matmul(h, w_in[l][:, offs[k]:offs[k + 1]].astype(BF16), seg_dtype[k])
                for k in range(len(sizes))]
        z, xbc, dt_raw, hq, hf, hi, hgate, rq, rk, rv, rgate, bgate = segs

        xc = _conv_silu(geom, xbc, m_conv_w[l], m_conv_b[l])
        dtt = dt_raw.reshape(geom.t_all // CHUNK, CHUNK, 2 * m_heads).transpose(0, 2, 1)
        bias = m_dt_bias[l].reshape(1, 2 * m_heads)
        neg_a = -jnp.exp(m_a_log[l].astype(F32)).reshape(1, 2 * m_heads)
        prm = (bias, neg_a, bias.T, neg_a.T, e_mat)
        s0 = state_ssm[:, l].reshape(bl, 2, M_GROUPS, hpg, m_state, m_headdim)
        s0 = s0.transpose(0, 1, 2, 4, 3, 5).reshape(bl, 2, M_GROUPS, m_state, gw)
        yf_c, yb_c, sfin = _ssd_scan(xc, dt_raw, dtt, prm, None, dims=ssd_dims, n_seq=bp, seq_len=seq, row_off=0)
        yf_l, yb_l = _ssd_scan(xc, dt_raw, dtt, prm, s0, dims=ssd_dims, n_seq=bl, seq_len=dec_seq, row_off=t_ctx)
        sfin = sfin.reshape(bp, 2, M_GROUPS, m_state, hpg, m_headdim).transpose(0, 1, 2, 4, 3, 5)
        out_ssm.append(sfin.reshape(bp, 2, m_heads, m_state, m_headdim))
        y_ssm = _group_rms_gate(geom, (yf_c, yb_c), (yf_l, yb_l),
                                z, 0, m_norm[l], gs=m_inner // M_GROUPS, gate_before=True,
                                skip=(xc, jnp.repeat(m_d[l], m_headdim)))

        lb = lower[l]
        prm = tuple(a.reshape(2, 1, hg_w) for a in (jnp.log(lb), jnp.log1p(-lb), 1.0 - lb))
        s0 = jnp.swapaxes(state_hgrn[:, l], -1, -2)
        of_c, ob_c, gfin = _gla_scan(hq, hi, hf, prm, None, heads=hg_heads, dk=hg_dk, dv=hg_dv,
                                     n_seq=bp, seq_len=seq, row_off=0)
        of_l, ob_l = _gla_scan(hq, hi, hf, prm, s0, heads=hg_heads, dk=hg_dk, dv=hg_dv,
                               n_seq=bl, seq_len=dec_seq, row_off=t_ctx)
        out_hg.append(jnp.swapaxes(gfin, -1, -2))
        y_hg = _group_rms_gate(geom, (of_c, ob_c), (of_l, ob_l),
                               hgate, 0, hg_norm[l], gs=hg_dk, gate_before=False)

        lg = jnp.broadcast_to(log_g[l][:, :, None, None], (2, ret_heads, 1, LANES))
        rf_c, rb_c, tfin = _ret_scan(rq, rk, rv, lg, None, None, heads=ret_heads, dk=ret_dk, dv=ret_dv,
                                     n_seq=bp, seq_len=seq, row_off=0)
        rf_l, rb_l = _ret_scan(rq, rk, rv, lg, rope, state_ret[:, l], heads=ret_heads, dk=ret_dk, dv=ret_dv,
                               n_seq=bl, seq_len=dec_seq, row_off=t_ctx)
        out_ret.append(tfin)
        y_ret = _group_rms_gate(geom, (rf_c, rb_c), (rf_l, rb_l),
                                rgate, 0, ret_norm[l], gs=ret_dk, gate_before=False)

        ps = _matmul(y_ssm, w_br_ssm[l].astype(BF16), F32)
        ph = _matmul(y_hg, w_br_hg[l].astype(BF16), F32)
        pr = _matmul(y_ret, w_br_ret[l].astype(BF16), F32)
        merged = _merge(geom, bgate, ps, ph, pr)
        mix = _matmul(merged, w_out[l].astype(BF16), F32)
        x, h = _ln_mod(geom, x, alpha, ln=(mix, modtab, 2, ln1_g[l], ln1_b[l]), mod=(modtab, 4, 3))

        q = _matmul(h, pk_query[l].astype(BF16), F32)
        route = _route(q, pk_keys[l].astype(BF16), topk=PK_TOPK)
        ff = _peer(h, peer_u[l].astype(BF16), peer_v[l].astype(BF16), route, heads=pk_heads, nk=n_keys)
        if l + 1 < depth:
            x, h = _ln_mod(geom, x, alpha, ln=(ff, modtab, 5, ln2_g[l], ln2_b[l]), mod=(modtabs[l + 1], 1, 0))
        else:
            x, = _ln_mod(geom, x, alpha, ln=(ff, modtab, 5, ln2_g[l], ln2_b[l]))

    dt_out = x_prompt.dtype
    y_p = x[:t_ctx].reshape(bp, seq, d_model)
    y_s = x[t_ctx:].reshape(bl, dec_seq, d_model)
    return (y_p, y_s, jnp.stack(out_ssm, axis=1).astype(dt_out), jnp.stack(out_hg, axis=1).astype(dt_out),
            jnp.stack(out_ret, axis=1).astype(dt_out))
```

```python
import functools
import math

import jax
import jax.numpy as jnp
from jax import lax
from jax.experimental import pallas as pl
from jax.experimental.pallas import tpu as pltpu

F32 = jnp.float32
BF16 = jnp.bfloat16

GRID_W = 64
M_GROUPS = 8
CHUNK = 64
HG_CHUNK = 32
PK_TOPK = 16
ROPE_BASE = 10000.0
EPS = 1e-6
N_MOD = 6

V7X_VMEM_BYTES = 64 * 2**20
VMEM_LIMIT = V7X_VMEM_BYTES - 8 * 2**20
LANES = 128
MXU_WIDTH = 256
MASKED = -1e30
GLA_HEADS_PER_STEP = 4
RET_HEADS_PER_STEP = 4


def _params(*sem):
    return pltpu.CompilerParams(dimension_semantics=sem, vmem_limit_bytes=VMEM_LIMIT)


def _silu(x):
    return x * jax.nn.sigmoid(x)


def _split3(a):
    hi = a.astype(BF16)
    r1 = a - hi.astype(F32)
    mid = r1.astype(BF16)
    lo = (r1 - mid.astype(F32)).astype(BF16)
    return hi, mid, lo


def _dot(a, b):
    return jnp.dot(a, b, preferred_element_type=F32)


def _dot_nt(a, b):
    return lax.dot_general(a, b, (((1,), (1,)), ((), ())), preferred_element_type=F32)


def _dot_tn(a, b):
    return lax.dot_general(a, b, (((0,), (0,)), ((), ())), preferred_element_type=F32)


def _exact_left(sel, a):
    hi, mid, lo = _split3(a)
    return _dot(sel, hi) + _dot(sel, mid) + _dot(sel, lo)


def _exact_right(a, sel):
    hi, mid, lo = _split3(a)
    return _dot(hi, sel) + _dot(mid, sel) + _dot(lo, sel)


class Geom:
    def __init__(self, n_ctx_seq, ctx_len, n_lat_seq, lat_len):
        self.n_ctx_seq, self.ctx_len, self.n_lat_seq, self.lat_len = n_ctx_seq, ctx_len, n_lat_seq, lat_len
        self.t_ctx = n_ctx_seq * ctx_len
        self.t_lat = n_lat_seq * lat_len
        self.t_all = self.t_ctx + self.t_lat
        self.rb = min(256, ctx_len)
        assert ctx_len % self.rb == 0 and lat_len % self.rb == 0
        self.ctx_blocks = self.t_ctx // self.rb
        self.ctx_bps = ctx_len // self.rb
        self.lat_bps = lat_len // self.rb

    def mod_row(self, i):
        return jnp.where(i < self.ctx_blocks, 0, 1 + (i - self.ctx_blocks) // self.lat_bps)


def _ada_kernel(c_ref, w_ref, b_ref, o_ref):
    a = _silu(c_ref[...]).astype(BF16)
    o_ref[...] = _dot(a, w_ref[...].astype(BF16)) + b_ref[...]


def _ada(cvec, w, b):
    r, d = cvec.shape
    n = w.shape[1]
    tn = _tile(n, 512)
    return pl.pallas_call(
        _ada_kernel, grid=(n // tn,),
        in_specs=[pl.BlockSpec((r, d), lambda j: (0, 0)),
                  pl.BlockSpec((d, tn), lambda j: (0, j)),
                  pl.BlockSpec((1, tn), lambda j: (0, j))],
        out_specs=pl.BlockSpec((r, tn), lambda j: (0, j)),
        out_shape=jax.ShapeDtypeStruct((r, n), F32),
        compiler_params=_params("parallel"), name="ada")(cvec, w, b.reshape(1, n))


def _ln_mod_kernel(*refs, alpha, has_ln, has_mod):
    refs = list(refs)
    x_ref = refs.pop(0)
    x = x_ref[...]
    if has_ln:
        mix_ref, gate_ref, g_ref, b_ref = refs[:4]
        refs = refs[4:]
    if has_mod:
        sc_ref, sh_ref = refs[:2]
        refs = refs[2:]
    if has_ln:
        y = alpha * x + gate_ref[0] * mix_ref[...]
        mu = jnp.mean(y, axis=-1, keepdims=True)
        yc = y - mu
        var = jnp.mean(yc * yc, axis=-1, keepdims=True)
        x = yc * lax.rsqrt(var + EPS) * g_ref[...] + b_ref[...]
        xo_ref = refs.pop(0)
        xo_ref[...] = x
    if has_mod:
        ho_ref = refs.pop(0)
        ho_ref[...] = (x * (1.0 + sc_ref[0]) + sh_ref[0]).astype(BF16)


def _ln_mod(geom, x, alpha, ln=None, mod=None):
    t, d = x.shape
    rb = geom.rb
    row = pl.BlockSpec((rb, d), lambda i: (i, 0))
    vec = pl.BlockSpec((1, d), lambda i: (0, 0))

    def slot(k):
        return pl.BlockSpec((1, 1, d), lambda i: (geom.mod_row(i) * N_MOD + k, 0, 0))

    args, specs, outs, ospecs = [x], [row], [], []
    if ln is not None:
        mix, tab, gk, g, b = ln
        args += [mix, tab, g.reshape(1, d), b.reshape(1, d)]
        specs += [row, slot(gk), vec, vec]
        outs.append(jax.ShapeDtypeStruct((t, d), F32))
        ospecs.append(row)
    if mod is not None:
        tab, sk, hk = mod
        args += [tab, tab]
        specs += [slot(sk), slot(hk)]
        outs.append(jax.ShapeDtypeStruct((t, d), BF16))
        ospecs.append(row)
    res = pl.pallas_call(
        functools.partial(_ln_mod_kernel, alpha=alpha, has_ln=ln is not None, has_mod=mod is not None),
        grid=(t // rb,), in_specs=specs, out_specs=ospecs, out_shape=outs,
        compiler_params=_params("parallel"), name="ln_mod")(*args)
    return res


def _mm_kernel(x_ref, w_ref, o_ref):
    o_ref[...] = _dot(x_ref[...], w_ref[...]).astype(o_ref.dtype)


def _tile(n, want):
    if n <= want:
        return n
    t = want
    while n % t:
        t //= 2
    return t


def _matmul(x, w, out_dtype, tm=1024, tn=512):
    m, k = x.shape
    n = w.shape[1]
    tm, tn = _tile(m, tm), _tile(n, tn)
    assert tm % 8 == 0 and tn % LANES == 0
    return pl.pallas_call(
        _mm_kernel, grid=(m // tm, n // tn),
        in_specs=[pl.BlockSpec((tm, k), lambda i, j: (i, 0)),
                  pl.BlockSpec((k, tn), lambda i, j: (0, j))],
        out_specs=pl.BlockSpec((tm, tn), lambda i, j: (i, j)),
        out_shape=jax.ShapeDtypeStruct((m, n), out_dtype),
        compiler_params=_params("parallel", "arbitrary"), name="matmul")(x, w)


CONV_HALO = 16


def _conv_kernel(prev_ref, cur_ref, next_ref, w_ref, b_ref, o_ref, *, geom, conv_w):
    i = pl.program_id(0)
    rb = geom.rb
    in_ctx = i < geom.ctx_blocks
    k_ctx = i % geom.ctx_bps
    k_lat = (i - geom.ctx_blocks) % geom.lat_bps
    is_start = jnp.where(in_ctx, k_ctx == 0, k_lat == 0)
    is_end = jnp.where(in_ctx, k_ctx == geom.ctx_bps - 1, k_lat == geom.lat_bps - 1)
    prev = jnp.where(is_start, 0.0, prev_ref[...].astype(F32))
    nxt = jnp.where(is_end, 0.0, next_ref[...].astype(F32))
    z = jnp.concatenate([prev, cur_ref[...].astype(F32), nxt], axis=0)
    rows = rb + 2 * CONV_HALO
    pad = conv_w // 2
    acc = jnp.zeros(o_ref.shape, F32) + b_ref[...]
    for k in range(conv_w):
        zk = z if k == pad else pltpu.roll(z, (pad - k) % rows, 0)
        acc = acc + w_ref[k:k + 1, :] * zk[CONV_HALO:CONV_HALO + rb]
    o_ref[...] = _silu(acc).astype(o_ref.dtype)


def _conv_silu(geom, x, w, b):
    t, c = x.shape
    rb = geom.rb
    cb = _tile(c, 512)
    assert rb % CONV_HALO == 0
    hb = rb // CONV_HALO
    last = t // CONV_HALO - 1
    conv_w = w.shape[0]
    return pl.pallas_call(
        functools.partial(_conv_kernel, geom=geom, conv_w=conv_w), grid=(t // rb, c // cb),
        in_specs=[pl.BlockSpec((CONV_HALO, cb), lambda i, j: (jnp.maximum(i * hb - 1, 0), j)),
                  pl.BlockSpec((rb, cb), lambda i, j: (i, j)),
                  pl.BlockSpec((CONV_HALO, cb), lambda i, j: (jnp.minimum((i + 1) * hb, last), j)),
                  pl.BlockSpec((conv_w, cb), lambda i, j: (0, j)),
                  pl.BlockSpec((1, cb), lambda i, j: (0, j))],
        out_specs=pl.BlockSpec((rb, cb), lambda i, j: (i, j)),
        out_shape=jax.ShapeDtypeStruct((t, c), BF16),
        compiler_params=_params("parallel", "parallel"), name="conv_silu")(x, x, x, w, b.reshape(1, c))


def _ssd_kernel(*refs, dims, has_init):
    h_all, g_all, n_state, p_dim = dims
    hpg = h_all // g_all
    inner = h_all * p_dim
    gw = hpg * p_dim
    L = CHUNK
    refs = list(refs)
    xc = refs[0:2]
    dt = refs[2:4]
    dtt = refs[4:6]
    brow, arow, bcol, acol, e_ref = refs[6:11]
    refs = refs[11:]
    if has_init:
        s0_ref = refs.pop(0)
    y_out = refs[0:2]
    refs = refs[2:]
    if not has_init:
        sfin_ref = refs.pop(0)
    s_scr, cumx_s, ecx_s, xdt_s, xs_s, cumt_s, elast_s = refs

    c = pl.program_id(1)

    @pl.when(c == 0)
    def _():
        if has_init:
            s_scr[...] = s0_ref[0]
        else:
            s_scr[...] = jnp.zeros(s_scr.shape, F32)

    ri = lax.broadcasted_iota(jnp.int32, (L, L), 0)
    ci = lax.broadcasted_iota(jnp.int32, (L, L), 1)
    e_mat = e_ref[...]

    for d in range(2):
        lo, hi = d * h_all, (d + 1) * h_all
        keep = (ci <= ri) if d == 0 else (ci >= ri)
        tri = keep.astype(BF16)
        dtv = jax.nn.softplus(dt[d][:, lo:hi] + brow[:, lo:hi])
        cum = _exact_left(tri, dtv * arow[:, lo:hi])
        both = _exact_right(jnp.concatenate([cum, dtv], axis=0), e_mat)
        cumx, dtx = both[:L], both[L:]
        last = cumx[L - 1:L, :] if d == 0 else cumx[0:1, :]
        xdt = xc[d][:, :inner].astype(F32) * dtx
        cumx_s[...] = cumx
        ecx_s[...] = jnp.exp(cumx)
        xdt_s[...] = xdt.astype(BF16)
        xs_s[...] = (xdt * jnp.exp(last - cumx)).astype(BF16)
        elast_s[...] = jnp.broadcast_to(jnp.exp(last), elast_s.shape)
        lat = jax.nn.softplus(dtt[d][0, lo:hi, :] + bcol[lo:hi, :]) * acol[lo:hi, :]
        keep_t = (ri <= ci) if d == 0 else (ri >= ci)
        cumt_s[...] = _exact_right(lat, keep_t.astype(BF16))

        def group(g, carry, d=d, keep=keep):
            off_b = pl.multiple_of(inner + g * n_state, n_state)
            off_c = pl.multiple_of(inner + (g_all + g) * n_state, n_state)
            off_g = pl.multiple_of(g * gw, gw)
            bg = xc[d][:, pl.ds(off_b, n_state)]
            cg = xc[d][:, pl.ds(off_c, n_state)]
            scores = _dot_nt(cg, bg)
            s_old = s_scr[d, g]
            inter = _dot(cg, s_old.astype(BF16)) * ecx_s[:, pl.ds(off_g, gw)]
            upd = _dot_tn(bg, xs_s[:, pl.ds(off_g, gw)])
            cum_g = cumx_s[:, pl.ds(off_g, gw)]
            xdt_g = xdt_s[:, pl.ds(off_g, gw)]
            masked = []
            for h in range(hpg):
                col = cum_g[:, h * p_dim:h * p_dim + 1]
                row = cumt_s[pl.ds(g * hpg + h, 1), :]
                dec = jnp.exp(jnp.where(keep, col - row, -jnp.inf))
                masked.append((scores * dec).astype(BF16))
            ys = [_dot(masked[h], xdt_g[:, h * p_dim:(h + 1) * p_dim]) for h in range(hpg)]
            y_out[d][:, pl.ds(off_g, gw)] = jnp.concatenate(ys, axis=1) + inter
            s_scr[d, g] = s_old * elast_s[0:1, pl.ds(off_g, gw)] + upd
            return carry

        lax.fori_loop(0, g_all, group, 0, unroll=4 if g_all % 4 == 0 else 1)

    if not has_init:
        @pl.when(c == pl.num_programs(1) - 1)
        def _():
            sfin_ref[0] = s_scr[...]


def _ssd_scan(xc, dt, dtt, prm, s0, *, dims, n_seq, seq_len, row_off):
    h_all, g_all, n_state, p_dim = dims
    inner, gw = h_all * p_dim, (h_all // g_all) * p_dim
    L = CHUNK
    nc = seq_len // L
    off = row_off // L
    cw = xc.shape[1]
    has_init = s0 is not None
    brow, arow, bcol, acol, e_mat = prm

    def fwd(b, c):
        return off + b * nc + c

    def bwd(b, c):
        return off + b * nc + (nc - 1 - c)

    full = lambda a: pl.BlockSpec(a.shape, lambda b, c: (0,) * a.ndim)
    in_specs = [pl.BlockSpec((L, cw), lambda b, c: (fwd(b, c), 0)),
                pl.BlockSpec((L, cw), lambda b, c: (bwd(b, c), 0)),
                pl.BlockSpec((L, 2 * h_all), lambda b, c: (fwd(b, c), 0)),
                pl.BlockSpec((L, 2 * h_all), lambda b, c: (bwd(b, c), 0)),
                pl.BlockSpec((1, 2 * h_all, L), lambda b, c: (fwd(b, c), 0, 0)),
                pl.BlockSpec((1, 2 * h_all, L), lambda b, c: (bwd(b, c), 0, 0)),
                full(brow), full(arow), full(bcol), full(acol), full(e_mat)]
    args = [xc, xc, dt, dt, dtt, dtt, brow, arow, bcol, acol, e_mat]
    st_block = (1, 2, g_all, n_state, gw)
    if has_init:
        in_specs.append(pl.BlockSpec(st_block, lambda b, c: (b, 0, 0, 0, 0)))
        args.append(s0)
    t_loc = n_seq * seq_len
    out_shape = [jax.ShapeDtypeStruct((t_loc, inner), F32)] * 2
    out_specs = [pl.BlockSpec((L, inner), lambda b, c: (b * nc + c, 0)),
                 pl.BlockSpec((L, inner), lambda b, c: (b * nc + (nc - 1 - c), 0))]
    if not has_init:
        out_shape.append(jax.ShapeDtypeStruct((n_seq,) + st_block[1:], F32))
        out_specs.append(pl.BlockSpec(st_block, lambda b, c: (b, 0, 0, 0, 0)))
    scratch = [pltpu.VMEM((2, g_all, n_state, gw), F32),
               pltpu.VMEM((L, inner), F32), pltpu.VMEM((L, inner), F32),
               pltpu.VMEM((L, inner), BF16), pltpu.VMEM((L, inner), BF16),
               pltpu.VMEM((h_all, L), F32), pltpu.VMEM((8, inner), F32)]
    return pl.pallas_call(
        functools.partial(_ssd_kernel, dims=dims, has_init=has_init), grid=(n_seq, nc),
        in_specs=in_specs, out_specs=out_specs, out_shape=out_shape, scratch_shapes=scratch,
        compiler_params=_params("parallel", "arbitrary"), name="ssd_scan")(*args)


def _gla_kernel(*refs, tb, dk, dv, hps, has_init):
    L = HG_CHUNK
    nch = tb // L
    mid = L // 2
    refs = list(refs)
    q_r, v_r, f_r = refs[0:2], refs[2:4], refs[4:6]
    loglb, log1m, onem = refs[6:8], refs[8:10], refs[10:12]
    refs = refs[12:]
    if has_init:
        s0_ref = refs.pop(0)
    o_r = refs[0:2]
    refs = refs[2:]
    if not has_init:
        sfin_ref = refs.pop(0)
    st_scr, = refs

    tblk = pl.program_id(2)

    @pl.when(tblk == 0)
    def _():
        if has_init:
            st_scr[...] = s0_ref[0]
        else:
            st_scr[...] = jnp.zeros(st_scr.shape, F32)

    ri = lax.broadcasted_iota(jnp.int32, (tb, tb), 0)
    ci = lax.broadcasted_iota(jnp.int32, (tb, tb), 1)
    same = (ri // L) == (ci // L)
    rl = lax.broadcasted_iota(jnp.int32, (L, L), 0)
    cl = lax.broadcasted_iota(jnp.int32, (L, L), 1)

    for d in range(2):
        hf = f_r[d][...]
        log_sig = jnp.minimum(hf, 0.0) - jnp.log1p(jnp.exp(-jnp.abs(hf)))
        a = loglb[d][0]
        b = log1m[d][0] + log_sig
        log_f = jnp.maximum(a, b) + jnp.log1p(jnp.exp(-jnp.abs(a - b)))
        k = onem[d][0] * jax.nn.sigmoid(-hf)
        q = _silu(q_r[d][...].astype(F32))
        v = v_r[d][...]
        tri_keep = same & ((ci <= ri) if d == 0 else (ci >= ri))
        cum = _exact_left(tri_keep.astype(BF16), log_f)
        ref_row = mid if d == 0 else mid - 1
        last_row = L - 1 if d == 0 else 0
        refx = jnp.concatenate([jnp.broadcast_to(cum[j * L + ref_row:j * L + ref_row + 1], (L, cum.shape[1]))
                                for j in range(nch)], axis=0)
        lastx = jnp.concatenate([jnp.broadcast_to(cum[j * L + last_row:j * L + last_row + 1], (L, cum.shape[1]))
                                 for j in range(nch)], axis=0)
        qd = (q * jnp.exp(cum - refx)).astype(BF16)
        kd = (k * jnp.exp(refx - cum)).astype(BF16)
        qs = (q * jnp.exp(cum)).astype(BF16)
        ks = (k * jnp.exp(lastx - cum)).astype(BF16)
        elast = jnp.exp(lastx)
        keep = (cl <= rl) if d == 0 else (cl >= rl)
        order = range(nch) if d == 0 else range(nch - 1, -1, -1)
        for hh in range(hps):
            kc = slice(hh * dk, (hh + 1) * dk)
            vc = slice(hh * dv, (hh + 1) * dv)
            rows = [slice(j * L, (j + 1) * L) for j in range(nch)]
            sc = [jnp.where(keep, _dot_nt(qd[sl, kc], kd[sl, kc]), 0.0).astype(BF16) for sl in rows]
            upd = [_dot_tn(v[sl, vc], ks[sl, kc]) for sl in rows]
            intra = [_dot(sc[j], v[rows[j], vc]) for j in range(nch)]
            st = st_scr[d, hh]
            for j in order:
                o_r[d][rows[j], vc] = intra[j] + _dot_nt(qs[rows[j], kc], st.astype(BF16))
                st = st * elast[j * L:j * L + 1, kc] + upd[j]
            st_scr[d, hh] = st

    if not has_init:
        @pl.when(tblk == pl.num_programs(2) - 1)
        def _():
            sfin_ref[0] = st_scr[...]


def _gla_scan(hq, hi, hf, prm, s0, *, heads, dk, dv, n_seq, seq_len, row_off):
    tb = min(256, seq_len)
    ntb = seq_len // tb
    off = row_off // tb
    has_init = s0 is not None
    loglb, log1m, onem = prm
    hps = GLA_HEADS_PER_STEP if heads % GLA_HEADS_PER_STEP == 0 else 1
    hblk = heads // hps

    def fwd(b, h, t):
        return off + b * ntb + t

    def bwd(b, h, t):
        return off + b * ntb + (ntb - 1 - t)

    def tok(rowf, col0):
        return pl.BlockSpec((tb, hps * dk), lambda b, h, t: (rowf(b, h, t), col0 + h))

    def par(d):
        return pl.BlockSpec((1, 1, hps * dk), lambda b, h, t: (d, 0, h))

    in_specs = [tok(fwd, 0), tok(bwd, 0), tok(fwd, 0), tok(bwd, 0), tok(fwd, 0), tok(bwd, hblk),
                par(0), par(1), par(0), par(1), par(0), par(1)]
    args = [hq, hq, hi, hi, hf, hf, loglb, loglb, log1m, log1m, onem, onem]
    st_block = (1, 2, hps, dv, dk)
    st_map = lambda b, h, t: (b, 0, h, 0, 0)
    if has_init:
        in_specs.append(pl.BlockSpec(st_block, st_map))
        args.append(s0)
    t_loc = n_seq * seq_len
    out_shape = [jax.ShapeDtypeStruct((t_loc, heads * dv), F32)] * 2
    out_specs = [pl.BlockSpec((tb, hps * dv), lambda b, h, t: (b * ntb + t, h)),
                 pl.BlockSpec((tb, hps * dv), lambda b, h, t: (b * ntb + (ntb - 1 - t), h))]
    if not has_init:
        out_shape.append(jax.ShapeDtypeStruct((n_seq, 2, heads, dv, dk), F32))
        out_specs.append(pl.BlockSpec(st_block, st_map))
    return pl.pallas_call(
        functools.partial(_gla_kernel, tb=tb, dk=dk, dv=dv, hps=hps, has_init=has_init), grid=(n_seq, hblk, ntb),
        in_specs=in_specs, out_specs=out_specs, out_shape=out_shape,
        scratch_shapes=[pltpu.VMEM((2, hps, dv, dk), F32)],
        compiler_params=_params("parallel", "parallel", "arbitrary"), name="gla_scan")(*args)


def _ret_kernel(*refs, tb, dk, hps, has_init, scale):
    L = CHUNK
    nch = tb // L
    refs = list(refs)
    q_r, k_r, v_r = refs[0:2], refs[2:4], refs[4:6]
    lg_r = refs[6:8]
    refs = refs[8:]
    if has_init:
        cos_r, sin_r = refs[0:2], refs[2:4]
        s0_ref = refs[4]
        refs = refs[5:]
    o_r = refs[0:2]
    refs = refs[2:]
    if not has_init:
        sfin_ref = refs.pop(0)
    s_scr, = refs

    tblk = pl.program_id(2)

    @pl.when(tblk == 0)
    def _():
        if has_init:
            s_scr[...] = s0_ref[0]
        else:
            s_scr[...] = jnp.zeros(s_scr.shape, F32)

    rl = lax.broadcasted_iota(jnp.int32, (L, L), 0)
    cl = lax.broadcasted_iota(jnp.int32, (L, L), 1)
    pos = lax.broadcasted_iota(jnp.int32, (L, dk), 0).astype(F32)

    def rope(x, cos, sin):
        parts = [pltpu.roll(x[:, i:i + LANES], LANES // 2, 1) for i in range(0, dk, LANES)]
        return x * cos + jnp.concatenate(parts, axis=1) * sin

    for d, hh in [(d, hh) for d in range(2) for hh in range(hps)]:
        cols = slice(hh * dk, (hh + 1) * dk)
        lg = lg_r[d][0, hh]
        lgk = jnp.concatenate([lg] * (dk // LANES), axis=1)
        q = q_r[d][:, cols].astype(F32)
        k = k_r[d][:, cols].astype(F32) * scale
        if has_init:
            q = rope(q, cos_r[d][...], sin_r[d][...])
            k = rope(k, cos_r[d][...], sin_r[d][...])
        q = q.astype(BF16)
        k = k.astype(BF16)
        v = v_r[d][:, cols]
        if d == 0:
            keep = cl <= rl
            dmat = jnp.exp(jnp.where(keep, (rl - cl).astype(F32) * lg[:, :L], -jnp.inf))
            w_q = jnp.exp((pos + 1.0) * lgk)
            w_k = jnp.exp((L - 1.0 - pos) * lgk)
        else:
            keep = cl >= rl
            dmat = jnp.exp(jnp.where(keep, (cl - rl).astype(F32) * lg[:, :L], -jnp.inf))
            w_q = jnp.exp((L - pos) * lgk)
            w_k = jnp.exp(pos * lgk)
        e_chunk = jnp.exp(float(L) * lgk)
        s = s_scr[d, hh]
        order = range(nch) if d == 0 else range(nch - 1, -1, -1)
        rows = [slice(j * L, (j + 1) * L) for j in range(nch)]
        sc = [(_dot_nt(q[sl], k[sl]) * dmat).astype(BF16) for sl in rows]
        upd = [_dot_tn((k[sl].astype(F32) * w_k).astype(BF16), v[sl]) for sl in rows]
        intra = [_dot(sc[j], v[rows[j]]) for j in range(nch)]
        for j in order:
            o_r[d][rows[j], cols] = intra[j] + _dot(q[rows[j]], s.astype(BF16)) * w_q
            s = s * e_chunk[:, :1] + upd[j]
        s_scr[d, hh] = s

    if not has_init:
        @pl.when(tblk == pl.num_programs(2) - 1)
        def _():
            sfin_ref[0] = s_scr[...]


def _ret_scan(rq, rk, rv, lg, rope, s0, *, heads, dk, dv, n_seq, seq_len, row_off):
    assert dk == dv
    tb = min(256, seq_len)
    ntb = seq_len // tb
    off = row_off // tb
    has_init = s0 is not None

    def fwd(b, h, t):
        return off + b * ntb + t

    def bwd(b, h, t):
        return off + b * ntb + (ntb - 1 - t)

    hps = RET_HEADS_PER_STEP if heads % RET_HEADS_PER_STEP == 0 else 1

    def tok(rowf):
        return pl.BlockSpec((tb, hps * dk), lambda b, h, t: (rowf(b, h, t), h))

    def lgs(d):
        return pl.BlockSpec((1, hps, 1, LANES), lambda b, h, t: (d, h, 0, 0))

    in_specs = [tok(fwd), tok(bwd), tok(fwd), tok(bwd), tok(fwd), tok(bwd), lgs(0), lgs(1)]
    args = [rq, rq, rk, rk, rv, rv, lg, lg]
    st_block = (1, 2, hps, dk, dv)
    st_map = lambda b, h, t: (b, 0, h, 0, 0)
    if has_init:
        cos, sin = rope
        pf = pl.BlockSpec((tb, dk), lambda b, h, t: (t, 0))
        pb = pl.BlockSpec((tb, dk), lambda b, h, t: (ntb - 1 - t, 0))
        in_specs += [pf, pb, pf, pb, pl.BlockSpec(st_block, st_map)]
        args += [cos, cos, sin, sin, s0]
    t_loc = n_seq * seq_len
    out_shape = [jax.ShapeDtypeStruct((t_loc, heads * dv), F32)] * 2
    out_specs = [pl.BlockSpec((tb, hps * dv), lambda b, h, t: (b * ntb + t, h)),
                 pl.BlockSpec((tb, hps * dv), lambda b, h, t: (b * ntb + (ntb - 1 - t), h))]
    if not has_init:
        out_shape.append(jax.ShapeDtypeStruct((n_seq, 2, heads, dk, dv), F32))
        out_specs.append(pl.BlockSpec(st_block, st_map))
    return pl.pallas_call(
        functools.partial(_ret_kernel, tb=tb, dk=dk, hps=hps, has_init=has_init, scale=dk ** -0.5),
        grid=(n_seq, heads // hps, ntb),
        in_specs=in_specs, out_specs=out_specs, out_shape=out_shape,
        scratch_shapes=[pltpu.VMEM((2, hps, dk, dv), F32)],
        compiler_params=_params("parallel", "parallel", "arbitrary"), name="ret_scan")(*args)


def _grms_kernel(*refs, gs, gate_before, has_skip, ctx_blocks):
    refs = list(refs)
    ac_ref, bc_ref, al_ref, bl_ref, gate_ref, w_ref = refs[:6]
    refs = refs[6:]
    if has_skip:
        x_ref, d_ref = refs[:2]
        refs = refs[2:]
    o_ref, = refs
    y = jnp.where(pl.program_id(0) < ctx_blocks, ac_ref[...] + bc_ref[...], al_ref[...] + bl_ref[...])
    if has_skip:
        y = y + d_ref[...] * x_ref[...].astype(F32)
    gate = _silu(gate_ref[...].astype(F32))
    if gate_before:
        y = y * gate
    width = y.shape[1]
    outs = []
    for g in range(width // gs):
        yg = y[:, g * gs:(g + 1) * gs]
        ms = jnp.mean(yg * yg, axis=-1, keepdims=True)
        outs.append(yg * lax.rsqrt(ms + EPS))
    yn = jnp.concatenate(outs, axis=1) * w_ref[...]
    if not gate_before:
        yn = yn * gate
    o_ref[...] = yn.astype(o_ref.dtype)


def _group_rms_gate(geom, ab_ctx, ab_lat, gate, gate_col0, w, *, gs, gate_before, skip=None):
    t = geom.t_all
    width = ab_ctx[0].shape[1]
    rb = geom.rb
    cb = max(_tile(width, 512), gs)
    assert width % cb == 0 and cb % gs == 0 and gate_col0 % cb == 0
    nctx, ncb = geom.ctx_blocks, width // cb
    row = pl.BlockSpec((rb, cb), lambda i, j: (i, j))
    vec = pl.BlockSpec((1, cb), lambda i, j: (0, j))
    ctx = pl.BlockSpec((rb, cb), lambda i, j: (jnp.minimum(i, nctx - 1), jnp.where(i < nctx, j, ncb - 1)))
    lat = pl.BlockSpec((rb, cb), lambda i, j: (jnp.maximum(i - nctx, 0), jnp.where(i < nctx, 0, j)))
    g0 = gate_col0 // cb
    in_specs = [ctx, ctx, lat, lat, pl.BlockSpec((rb, cb), lambda i, j: (i, g0 + j)), vec]
    args = [*ab_ctx, *ab_lat, gate, w.reshape(1, width)]
    if skip is not None:
        x, dvec = skip
        in_specs += [row, vec]
        args += [x, dvec.reshape(1, width)]
    return pl.pallas_call(
        functools.partial(_grms_kernel, gs=gs, gate_before=gate_before, has_skip=skip is not None, ctx_blocks=nctx),
        grid=(t // rb, width // cb), in_specs=in_specs, out_specs=row,
        out_shape=jax.ShapeDtypeStruct((t, width), BF16),
        compiler_params=_params("parallel", "parallel"), name="group_rms_gate")(*args)


def _merge_kernel(gs_ref, gh_ref, gr_ref, ys_ref, yh_ref, yr_ref, ws_ref, wh_ref, wr_ref, o_ref):
    acc = jax.nn.sigmoid(gs_ref[...].astype(F32)) * _dot(ys_ref[...], ws_ref[...])
    acc = acc + jax.nn.sigmoid(gh_ref[...].astype(F32)) * _dot(yh_ref[...], wh_ref[...])
    acc = acc + jax.nn.sigmoid(gr_ref[...].astype(F32)) * _dot(yr_ref[...], wr_ref[...])
    o_ref[...] = acc.astype(o_ref.dtype)


def _branch_merge(bgate, ys, yh, yr, ws, wh, wr, tm=512, tn=512):
    t = ys.shape[0]
    d = ws.shape[1]
    tm, tn = _tile(t, tm), _tile(d, tn)
    nb = d // tn
    gate = lambda k: pl.BlockSpec((tm, tn), lambda i, j: (i, k * nb + j))
    act = lambda y: pl.BlockSpec((tm, y.shape[1]), lambda i, j: (i, 0), pipeline_mode=pl.Buffered(1))
    wgt = lambda w: pl.BlockSpec((w.shape[0], tn), lambda i, j: (0, j))
    return pl.pallas_call(
        _merge_kernel, grid=(t // tm, nb),
        in_specs=[gate(0), gate(1), gate(2), act(ys), act(yh), act(yr), wgt(ws), wgt(wh), wgt(wr)],
        out_specs=pl.BlockSpec((tm, tn), lambda i, j: (i, j)), out_shape=jax.ShapeDtypeStruct((t, d), BF16),
        compiler_params=_params("parallel", "arbitrary"), name="branch_merge")(
            bgate, bgate, bgate, ys, yh, yr, ws, wh, wr)


def _topk_desc(s, k):
    rows = []
    cur = s
    for _ in range(k):
        m = jnp.max(cur, axis=0, keepdims=True)
        rows.append(m)
        cur = jnp.where(cur == m, -jnp.inf, cur)
    return rows


def _route_kernel(q_ref, keys_ref, a1_ref, s1_ref, a2_ref, s2_ref, tau_ref, *, heads, nk, half, topk):
    tm = q_ref.shape[0]
    row_k = lax.broadcasted_iota(jnp.int32, (topk, tm), 0)
    row_t = lax.broadcasted_iota(jnp.int32, (tau_ref.shape[0], tm), 0)
    tau_all = jnp.zeros(tau_ref.shape, F32)
    for h in range(heads):
        tops, masked = [], []
        for c in range(2):
            qh = q_ref[:, (2 * h + c) * half:(2 * h + c + 1) * half].astype(BF16)
            s = _dot_nt(keys_ref[h, c], qh)
            rows = _topk_desc(s, topk)
            tops.append(rows)
            masked.append(jnp.where(s >= rows[-1], s, MASKED))
        v2 = jnp.zeros((topk, tm), F32)
        for b in range(topk):
            v2 = jnp.where(row_k == b, tops[1][b], v2)
        cand = jnp.concatenate([tops[0][a] + v2 for a in range(topk)], axis=0)
        best = _topk_desc(cand, topk)
        tau = best[-1]
        z = sum(jnp.exp(b - best[0]) for b in best)
        inv_z = 1.0 / z
        a1_ref[h * nk:(h + 1) * nk, :] = jnp.exp(masked[0] - tops[0][0]) * inv_z
        a2_ref[h * nk:(h + 1) * nk, :] = jnp.exp(masked[1] - tops[1][0])
        s1_ref[h * nk:(h + 1) * nk, :] = masked[0]
        s2_ref[h * nk:(h + 1) * nk, :] = masked[1]
        tau_all = jnp.where(row_t == h, tau, tau_all)
    tau_ref[...] = tau_all


def _route(q, keys, *, topk):
    t = q.shape[0]
    heads, _, nk, half = keys.shape
    tm = _tile(t, 256)
    assert tm % LANES == 0
    tau_rows = 8 * ((heads + 7) // 8)
    col = lambda r: pl.BlockSpec((r, tm), lambda i: (0, i))
    wide = jax.ShapeDtypeStruct((heads * nk, t), F32)
    return pl.pallas_call(
        functools.partial(_route_kernel, heads=heads, nk=nk, half=half, topk=topk), grid=(t // tm,),
        in_specs=[pl.BlockSpec((tm, q.shape[1]), lambda i: (i, 0)), pl.BlockSpec(keys.shape, lambda i: (0, 0, 0, 0))],
        out_specs=[col(heads * nk)] * 4 + [col(tau_rows)],
        out_shape=[wide] * 4 + [jax.ShapeDtypeStruct((tau_rows, t), F32)],
        compiler_params=_params("parallel"), name="peer_route")(q, keys)


def _peer_kernel(x_ref, u_ref, v_ref, a1_ref, s1_ref, a2_ref, s2_ref, tau_ref, o_ref, w_even, w_odd, *,
                 heads, nk, n_tiles):
    j = pl.program_id(1)
    tn = u_ref.shape[0]
    nsub = tn // nk
    tile = jnp.minimum(j, n_tiles - 1)

    @pl.when(j == 0)
    def _():
        w_even[...] = jnp.zeros(w_even.shape, BF16)
        o_ref[...] = jnp.zeros(o_ref.shape, F32)

    def step(w_prev, w_next):
        o_ref[...] += _dot(w_prev[...], v_ref[...])
        x = x_ref[...]
        gates = []
        for e in range(nsub):
            e1 = tile * nsub + e
            gate_t = jnp.zeros((nk, x.shape[0]), F32)
            for h in range(heads):
                s1 = s1_ref[pl.ds(h * nk + e1, 1), :]
                a1 = a1_ref[pl.ds(h * nk + e1, 1), :]
                hit = (s1 + s2_ref[h * nk:(h + 1) * nk, :]) >= tau_ref[h:h + 1, :]
                gate_t = gate_t + jnp.where(hit, a1 * a2_ref[h * nk:(h + 1) * nk, :], 0.0)
            gates.append(gate_t.T)
        wide = min(tn, MXU_WIDTH)
        for p in range(tn // wide):
            pre = _dot_nt(x, u_ref[p * wide:(p + 1) * wide, :])
            act = 0.5 * pre * (1.0 + lax.erf(pre * (1.0 / math.sqrt(2.0))))
            for q in range(wide // nk):
                blk = act[:, q * nk:(q + 1) * nk] * gates[p * (wide // nk) + q]
                w_next[:, p * wide + q * nk:p * wide + (q + 1) * nk] = blk.astype(BF16)

    @pl.when(j % 2 == 0)
    def _():
        step(w_even, w_odd)

    @pl.when(j % 2 == 1)
    def _():
        step(w_odd, w_even)


def _peer(x, u, v, route, *, heads, nk, tm=512, tn=512):
    t, d = x.shape
    e_all = u.shape[0]
    tm, tn = _tile(t, tm), _tile(e_all, tn)
    assert tn % nk == 0 and tm % LANES == 0
    a1, s1, a2, s2, tau = route
    once = dict(pipeline_mode=pl.Buffered(1))
    nt = e_all // tn
    rowx = pl.BlockSpec((tm, d), lambda i, j: (i, 0), **once)
    tab_u = pl.BlockSpec((tn, d), lambda i, j: (jnp.minimum(j, nt - 1), 0))
    tab_v = pl.BlockSpec((tn, d), lambda i, j: (jnp.maximum(j - 1, 0), 0))
    wide = pl.BlockSpec((heads * nk, tm), lambda i, j: (0, i), **once)
    taus = pl.BlockSpec((tau.shape[0], tm), lambda i, j: (0, i), **once)
    return pl.pallas_call(
        functools.partial(_peer_kernel, heads=heads, nk=nk, n_tiles=nt), grid=(t // tm, nt + 1),
        in_specs=[rowx, tab_u, tab_v, wide, wide, wide, wide, taus],
        out_specs=pl.BlockSpec((tm, d), lambda i, j: (i, 0)), out_shape=jax.ShapeDtypeStruct((t, d), F32),
        scratch_shapes=[pltpu.VMEM((tm, tn), BF16), pltpu.VMEM((tm, tn), BF16)],
        compiler_params=_params("parallel", "arbitrary"), name="peer_dense")(x, u, v, a1, s1, a2, s2, tau)


def _rope_tables(seq_len, dk):
    half, quarter = dk // 2, dk // 4
    freqs = ROPE_BASE ** (-jnp.arange(quarter, dtype=F32) / quarter)
    t = jnp.arange(seq_len)
    ang_r = (t // GRID_W).astype(F32)[:, None] * freqs
    ang_c = (t % GRID_W).astype(F32)[:, None] * freqs
    cos = jnp.concatenate([jnp.cos(ang_r)] * 2 + [jnp.cos(ang_c)] * 2, axis=1)
    sin = jnp.concatenate([-jnp.sin(ang_r), jnp.sin(ang_r), -jnp.sin(ang_c), jnp.sin(ang_c)], axis=1)
    return cos, sin


def kernel(x_prompt, x_sample, c, state_ssm, state_hgrn, state_ret, c_ctx, w_ada, b_ada, w_in, m_conv_w, m_conv_b, m_dt_bias, m_a_log, m_d, m_norm, hg_lower_bounds, hg_norm, ret_decay, ret_norm, w_br_ssm, w_br_hg, w_br_ret, w_out, ln1_g, ln1_b, ln2_g, ln2_b, pk_query, pk_keys, peer_u, peer_v):
    bp, seq, d_model = x_prompt.shape
    bl, dec_seq, _ = x_sample.shape
    depth = w_in.shape[0]
    m_heads, m_state, m_headdim = state_ssm.shape[3:]
    hg_heads, hg_dk, hg_dv = state_hgrn.shape[3:]
    ret_heads, ret_dk, ret_dv = state_ret.shape[3:]
    m_inner = m_heads * m_headdim
    m_bc = M_GROUPS * m_state
    hg_w = hg_heads * hg_dk
    ret_w = ret_heads * ret_dk
    pk_heads, _, n_keys, pk_half = pk_keys.shape[1:]
    hpg = m_heads // M_GROUPS
    gw = hpg * m_headdim
    alpha = (2 * depth) ** 0.25
    geom = Geom(bp, seq, bl, dec_seq)
    t_ctx = geom.t_ctx
    ssd_dims = (m_heads, M_GROUPS, m_state, m_headdim)

    sizes = (m_inner, m_inner + 2 * m_bc, 2 * m_heads, hg_w, 2 * hg_w, hg_w, hg_w, ret_w, ret_w, ret_w, ret_w, 3 * d_model)
    offs = [0]
    for s in sizes:
        offs.append(offs[-1] + s)
    seg_dtype = (BF16, BF16, F32, BF16, F32, BF16, BF16, BF16, BF16, BF16, BF16, BF16)

    x = jnp.concatenate([x_prompt.reshape(t_ctx, d_model), x_sample.reshape(geom.t_lat, d_model)], axis=0)
    rows = 8 * ((1 + bl + 7) // 8)
    cvec = jnp.zeros((rows, d_model), F32).at[0].set(c_ctx).at[1:1 + bl].set(c)

    p_lb = jax.nn.softmax(hg_lower_bounds.astype(F32), axis=0)
    lower = jnp.cumsum(p_lb, axis=0) - p_lb[:1]
    rope = _rope_tables(dec_seq, ret_dk)
    e_mat = jnp.repeat(jnp.eye(m_heads, dtype=BF16), m_headdim, axis=1)
    log_g = jax.nn.log_sigmoid(ret_decay.astype(F32))

    out_ssm, out_hg, out_ret = [], [], []
    h = None
    modtabs = [_ada(cvec, w_ada[l], b_ada[l]).reshape(rows * N_MOD, 1, d_model) for l in range(depth)]
    h, = _ln_mod(geom, x, alpha, mod=(modtabs[0], 1, 0))
    for l in range(depth):
        modtab = modtabs[l]

        segs = [_matmul(h, w_in[l][:, offs[k]:offs[k + 1]].astype(BF16), seg_dtype[k])
                for k in range(len(sizes))]
        z, xbc, dt_raw, hq, hf, hi, hgate, rq, rk, rv, rgate, bgate = segs

        xc = _conv_silu(geom, xbc, m_conv_w[l], m_conv_b[l])
        dtt = dt_raw.reshape(geom.t_all // CHUNK, CHUNK, 2 * m_heads).transpose(0, 2, 1)
        bias = m_dt_bias[l].reshape(1, 2 * m_heads)
        neg_a = -jnp.exp(m_a_log[l].astype(F32)).reshape(1, 2 * m_heads)
        prm = (bias, neg_a, bias.T, neg_a.T, e_mat)
        s0 = state_ssm[:, l].reshape(bl, 2, M_GROUPS, hpg, m_state, m_headdim)
        s0 = s0.transpose(0, 1, 2, 4, 3, 5).reshape(bl, 2, M_GROUPS, m_state, gw)
        yf_c, yb_c, sfin = _ssd_scan(xc, dt_raw, dtt, prm, None, dims=ssd_dims, n_seq=bp, seq_len=seq, row_off=0)
        yf_l, yb_l = _ssd_scan(xc, dt_raw, dtt, prm, s0, dims=ssd_dims, n_seq=bl, seq_len=dec_seq, row_off=t_ctx)
        sfin = sfin.reshape(bp, 2, M_GROUPS, m_state, hpg, m_headdim).transpose(0, 1, 2, 4, 3, 5)
        out_ssm.append(sfin.reshape(bp, 2, m_heads, m_state, m_headdim))
        y_ssm = _group_rms_gate(geom, (yf_c, yb_c), (yf_l, yb_l),
                                z, 0, m_norm[l], gs=m_inner // M_GROUPS, gate_before=True,
                                skip=(xc, jnp.repeat(m_d[l], m_headdim)))

        lb = lower[l]
        prm = tuple(a.reshape(2, 1, hg_w) for a in (jnp.log(lb), jnp.log1p(-lb), 1.0 - lb))
        s0 = jnp.swapaxes(state_hgrn[:, l], -1, -2)
        of_c, ob_c, gfin = _gla_scan(hq, hi, hf, prm, None, heads=hg_heads, dk=hg_dk, dv=hg_dv,
                                     n_seq=bp, seq_len=seq, row_off=0)
        of_l, ob_l = _gla_scan(hq, hi, hf, prm, s0, heads=hg_heads, dk=hg_dk, dv=hg_dv,
                               n_seq=bl, seq_len=dec_seq, row_off=t_ctx)
        out_hg.append(jnp.swapaxes(gfin, -1, -2))
        y_hg = _group_rms_gate(geom, (of_c, ob_c), (of_l, ob_l),
                               hgate, 0, hg_norm[l], gs=hg_dk, gate_before=False)

        lg = jnp.broadcast_to(log_g[l][:, :, None, None], (2, ret_heads, 1, LANES))
        rf_c, rb_c, tfin = _ret_scan(rq, rk, rv, lg, None, None, heads=ret_heads, dk=ret_dk, dv=ret_dv,
                                     n_seq=bp, seq_len=seq, row_off=0)
        rf_l, rb_l = _ret_scan(rq, rk, rv, lg, rope, state_ret[:, l], heads=ret_heads, dk=ret_dk, dv=ret_dv,
                               n_seq=bl, seq_len=dec_seq, row_off=t_ctx)
        out_ret.append(tfin)
        y_ret = _group_rms_gate(geom, (rf_c, rb_c), (rf_l, rb_l),
                                rgate, 0, ret_norm[l], gs=ret_dk, gate_before=False)

        merged = _branch_merge(bgate, y_ssm, y_hg, y_ret, w_br_ssm[l].astype(BF16), w_br_hg[l].astype(BF16),
                               w_br_ret[l].astype(BF16))
        mix = _matmul(merged, w_out[l].astype(BF16), F32)
        x, h = _ln_mod(geom, x, alpha, ln=(mix, modtab, 2, ln1_g[l], ln1_b[l]), mod=(modtab, 4, 3))

        q = _matmul(h, pk_query[l].astype(BF16), F32)
        route = _route(q, pk_keys[l].astype(BF16), topk=PK_TOPK)
        ff = _peer(h, peer_u[l].astype(BF16), peer_v[l].astype(BF16), route, heads=pk_heads, nk=n_keys)
        if l + 1 < depth:
            x, h = _ln_mod(geom, x, alpha, ln=(ff, modtab, 5, ln2_g[l], ln2_b[l]), mod=(modtabs[l + 1], 1, 0))
        else:
            x, = _ln_mod(geom, x, alpha, ln=(ff, modtab, 5, ln2_g[l], ln2_b[l]))

    dt_out = x_prompt.dtype
    y_p = x[:t_ctx].reshape(bp, seq, d_model)
    y_s = x[t_ctx:].reshape(bl, dec_seq, d_model)
    return (y_p, y_s, jnp.stack(out_ssm, axis=1).astype(dt_out), jnp.stack(out_hg, axis=1).astype(dt_out),
            jnp.stack(out_ret, axis=1).astype(dt_out))
```

```python
import functools
import math

import jax
import jax.numpy as jnp
from jax import lax
from jax.experimental import pallas as pl
from jax.experimental.pallas import tpu as pltpu

F32 = jnp.float32
BF16 = jnp.bfloat16

GRID_W = 64
M_GROUPS = 8
CHUNK = 64
HG_CHUNK = 32
PK_TOPK = 16
ROPE_BASE = 10000.0
EPS = 1e-6
N_MOD = 6

V7X_VMEM_BYTES = 64 * 2**20
VMEM_LIMIT = V7X_VMEM_BYTES - 8 * 2**20
LANES = 128
MXU_WIDTH = 256
MASKED = -1e30
GLA_HEADS_PER_STEP = 4
RET_HEADS_PER_STEP = 4


def _params(*sem):
    return pltpu.CompilerParams(dimension_semantics=sem, vmem_limit_bytes=VMEM_LIMIT)


def _silu(x):
    return x * jax.nn.sigmoid(x)


def _split3(a):
    hi = a.astype(BF16)
    r1 = a - hi.astype(F32)
    mid = r1.astype(BF16)
    lo = (r1 - mid.astype(F32)).astype(BF16)
    return hi, mid, lo


def _dot(a, b):
    return jnp.dot(a, b, preferred_element_type=F32)


def _dot_nt(a, b):
    return lax.dot_general(a, b, (((1,), (1,)), ((), ())), preferred_element_type=F32)


def _dot_tn(a, b):
    return lax.dot_general(a, b, (((0,), (0,)), ((), ())), preferred_element_type=F32)


def _exact_left(sel, a):
    hi, mid, lo = _split3(a)
    return _dot(sel, hi) + _dot(sel, mid) + _dot(sel, lo)


def _exact_right(a, sel):
    hi, mid, lo = _split3(a)
    return _dot(hi, sel) + _dot(mid, sel) + _dot(lo, sel)


class Geom:
    def __init__(self, n_ctx_seq, ctx_len, n_lat_seq, lat_len):
        self.n_ctx_seq, self.ctx_len, self.n_lat_seq, self.lat_len = n_ctx_seq, ctx_len, n_lat_seq, lat_len
        self.t_ctx = n_ctx_seq * ctx_len
        self.t_lat = n_lat_seq * lat_len
        self.t_all = self.t_ctx + self.t_lat
        self.rb = min(256, ctx_len)
        assert ctx_len % self.rb == 0 and lat_len % self.rb == 0
        self.ctx_blocks = self.t_ctx // self.rb
        self.ctx_bps = ctx_len // self.rb
        self.lat_bps = lat_len // self.rb

    def mod_row(self, i):
        return jnp.where(i < self.ctx_blocks, 0, 1 + (i - self.ctx_blocks) // self.lat_bps)


def _ada_kernel(c_ref, w_ref, b_ref, o_ref):
    a = _silu(c_ref[...]).astype(BF16)
    o_ref[...] = _dot(a, w_ref[...].astype(BF16)) + b_ref[...]


def _ada(cvec, w, b):
    r, d = cvec.shape
    n = w.shape[1]
    tn = _tile(n, 512)
    return pl.pallas_call(
        _ada_kernel, grid=(n // tn,),
        in_specs=[pl.BlockSpec((r, d), lambda j: (0, 0)),
                  pl.BlockSpec((d, tn), lambda j: (0, j)),
                  pl.BlockSpec((1, tn), lambda j: (0, j))],
        out_specs=pl.BlockSpec((r, tn), lambda j: (0, j)),
        out_shape=jax.ShapeDtypeStruct((r, n), F32),
        compiler_params=_params("parallel"), name="ada")(cvec, w, b.reshape(1, n))


def _ln_mod_kernel(*refs, alpha, has_ln, has_mod):
    refs = list(refs)
    x_ref = refs.pop(0)
    x = x_ref[...]
    if has_ln:
        mix_ref, gate_ref, g_ref, b_ref = refs[:4]
        refs = refs[4:]
    if has_mod:
        sc_ref, sh_ref = refs[:2]
        refs = refs[2:]
    if has_ln:
        y = alpha * x + gate_ref[0] * mix_ref[...]
        mu = jnp.mean(y, axis=-1, keepdims=True)
        yc = y - mu
        var = jnp.mean(yc * yc, axis=-1, keepdims=True)
        x = yc * lax.rsqrt(var + EPS) * g_ref[...] + b_ref[...]
        xo_ref = refs.pop(0)
        xo_ref[...] = x
    if has_mod:
        ho_ref = refs.pop(0)
        ho_ref[...] = (x * (1.0 + sc_ref[0]) + sh_ref[0]).astype(BF16)


def _ln_mod(geom, x, alpha, ln=None, mod=None):
    t, d = x.shape
    rb = geom.rb
    row = pl.BlockSpec((rb, d), lambda i: (i, 0))
    vec = pl.BlockSpec((1, d), lambda i: (0, 0))

    def slot(k):
        return pl.BlockSpec((1, 1, d), lambda i: (geom.mod_row(i) * N_MOD + k, 0, 0))

    args, specs, outs, ospecs = [x], [row], [], []
    if ln is not None:
        mix, tab, gk, g, b = ln
        args += [mix, tab, g.reshape(1, d), b.reshape(1, d)]
        specs += [row, slot(gk), vec, vec]
        outs.append(jax.ShapeDtypeStruct((t, d), F32))
        ospecs.append(row)
    if mod is not None:
        tab, sk, hk = mod
        args += [tab, tab]
        specs += [slot(sk), slot(hk)]
        outs.append(jax.ShapeDtypeStruct((t, d), BF16))
        ospecs.append(row)
    res = pl.pallas_call(
        functools.partial(_ln_mod_kernel, alpha=alpha, has_ln=ln is not None, has_mod=mod is not None),
        grid=(t // rb,), in_specs=specs, out_specs=ospecs, out_shape=outs,
        compiler_params=_params("parallel"), name="ln_mod")(*args)
    return res


def _mm_kernel(x_ref, w_ref, o_ref):
    o_ref[...] = _dot(x_ref[...], w_ref[...]).astype(o_ref.dtype)


def _tile(n, want):
    if n <= want:
        return n
    t = want
    while n % t:
        t //= 2
    return t


def _matmul(x, w, out_dtype, tm=1024, tn=512):
    m, k = x.shape
    n = w.shape[1]
    tm, tn = _tile(m, tm), _tile(n, tn)
    assert tm % 8 == 0 and tn % LANES == 0
    return pl.pallas_call(
        _mm_kernel, grid=(m // tm, n // tn),
        in_specs=[pl.BlockSpec((tm, k), lambda i, j: (i, 0)),
                  pl.BlockSpec((k, tn), lambda i, j: (0, j))],
        out_specs=pl.BlockSpec((tm, tn), lambda i, j: (i, j)),
        out_shape=jax.ShapeDtypeStruct((m, n), out_dtype),
        compiler_params=_params("parallel", "arbitrary"), name="matmul")(x, w)


CONV_HALO = 16


def _conv_kernel(prev_ref, cur_ref, next_ref, w_ref, b_ref, o_ref, *, geom, conv_w):
    i = pl.program_id(0)
    rb = geom.rb
    in_ctx = i < geom.ctx_blocks
    k_ctx = i % geom.ctx_bps
    k_lat = (i - geom.ctx_blocks) % geom.lat_bps
    is_start = jnp.where(in_ctx, k_ctx == 0, k_lat == 0)
    is_end = jnp.where(in_ctx, k_ctx == geom.ctx_bps - 1, k_lat == geom.lat_bps - 1)
    prev = jnp.where(is_start, 0.0, prev_ref[...].astype(F32))
    nxt = jnp.where(is_end, 0.0, next_ref[...].astype(F32))
    z = jnp.concatenate([prev, cur_ref[...].astype(F32), nxt], axis=0)
    rows = rb + 2 * CONV_HALO
    pad = conv_w // 2
    acc = jnp.zeros(o_ref.shape, F32) + b_ref[...]
    for k in range(conv_w):
        zk = z if k == pad else pltpu.roll(z, (pad - k) % rows, 0)
        acc = acc + w_ref[k:k + 1, :] * zk[CONV_HALO:CONV_HALO + rb]
    o_ref[...] = _silu(acc).astype(o_ref.dtype)


def _conv_silu(geom, x, w, b):
    t, c = x.shape
    rb = geom.rb
    cb = _tile(c, 512)
    assert rb % CONV_HALO == 0
    hb = rb // CONV_HALO
    last = t // CONV_HALO - 1
    conv_w = w.shape[0]
    return pl.pallas_call(
        functools.partial(_conv_kernel, geom=geom, conv_w=conv_w), grid=(t // rb, c // cb),
        in_specs=[pl.BlockSpec((CONV_HALO, cb), lambda i, j: (jnp.maximum(i * hb - 1, 0), j)),
                  pl.BlockSpec((rb, cb), lambda i, j: (i, j)),
                  pl.BlockSpec((CONV_HALO, cb), lambda i, j: (jnp.minimum((i + 1) * hb, last), j)),
                  pl.BlockSpec((conv_w, cb), lambda i, j: (0, j)),
                  pl.BlockSpec((1, cb), lambda i, j: (0, j))],
        out_specs=pl.BlockSpec((rb, cb), lambda i, j: (i, j)),
        out_shape=jax.ShapeDtypeStruct((t, c), BF16),
        compiler_params=_params("parallel", "parallel"), name="conv_silu")(x, x, x, w, b.reshape(1, c))


def _ssd_kernel(*refs, dims, has_init):
    h_all, g_all, n_state, p_dim = dims
    hpg = h_all // g_all
    inner = h_all * p_dim
    gw = hpg * p_dim
    L = CHUNK
    refs = list(refs)
    xc = refs[0:2]
    dt = refs[2:4]
    dtt = refs[4:6]
    brow, arow, bcol, acol, e_ref = refs[6:11]
    refs = refs[11:]
    if has_init:
        s0_ref = refs.pop(0)
    y_out = refs[0:2]
    refs = refs[2:]
    if not has_init:
        sfin_ref = refs.pop(0)
    s_scr, cumx_s, ecx_s, xdt_s, xs_s, cumt_s, elast_s = refs

    c = pl.program_id(1)

    @pl.when(c == 0)
    def _():
        if has_init:
            s_scr[...] = s0_ref[0]
        else:
            s_scr[...] = jnp.zeros(s_scr.shape, F32)

    ri = lax.broadcasted_iota(jnp.int32, (L, L), 0)
    ci = lax.broadcasted_iota(jnp.int32, (L, L), 1)
    e_mat = e_ref[...]

    for d in range(2):
        lo, hi = d * h_all, (d + 1) * h_all
        keep = (ci <= ri) if d == 0 else (ci >= ri)
        tri = keep.astype(BF16)
        dtv = jax.nn.softplus(dt[d][:, lo:hi] + brow[:, lo:hi])
        cum = _exact_left(tri, dtv * arow[:, lo:hi])
        both = _exact_right(jnp.concatenate([cum, dtv], axis=0), e_mat)
        cumx, dtx = both[:L], both[L:]
        last = cumx[L - 1:L, :] if d == 0 else cumx[0:1, :]
        xdt = xc[d][:, :inner].astype(F32) * dtx
        cumx_s[...] = cumx
        ecx_s[...] = jnp.exp(cumx)
        xdt_s[...] = xdt.astype(BF16)
        xs_s[...] = (xdt * jnp.exp(last - cumx)).astype(BF16)
        elast_s[...] = jnp.broadcast_to(jnp.exp(last), elast_s.shape)
        lat = jax.nn.softplus(dtt[d][0, lo:hi, :] + bcol[lo:hi, :]) * acol[lo:hi, :]
        keep_t = (ri <= ci) if d == 0 else (ri >= ci)
        cumt_s[...] = _exact_right(lat, keep_t.astype(BF16))

        def group(g, carry, d=d, keep=keep):
            off_b = pl.multiple_of(inner + g * n_state, n_state)
            off_c = pl.multiple_of(inner + (g_all + g) * n_state, n_state)
            off_g = pl.multiple_of(g * gw, gw)
            bg = xc[d][:, pl.ds(off_b, n_state)]
            cg = xc[d][:, pl.ds(off_c, n_state)]
            scores = _dot_nt(cg, bg)
            s_old = s_scr[d, g]
            inter = _dot(cg, s_old.astype(BF16)) * ecx_s[:, pl.ds(off_g, gw)]
            upd = _dot_tn(bg, xs_s[:, pl.ds(off_g, gw)])
            cum_g = cumx_s[:, pl.ds(off_g, gw)]
            xdt_g = xdt_s[:, pl.ds(off_g, gw)]
            masked = []
            for h in range(hpg):
                col = cum_g[:, h * p_dim:h * p_dim + 1]
                row = cumt_s[pl.ds(g * hpg + h, 1), :]
                dec = jnp.exp(jnp.where(keep, col - row, -jnp.inf))
                masked.append((scores * dec).astype(BF16))
            ys = [_dot(masked[h], xdt_g[:, h * p_dim:(h + 1) * p_dim]) for h in range(hpg)]
            y_out[d][:, pl.ds(off_g, gw)] = jnp.concatenate(ys, axis=1) + inter
            s_scr[d, g] = s_old * elast_s[0:1, pl.ds(off_g, gw)] + upd
            return carry

        lax.fori_loop(0, g_all, group, 0, unroll=4 if g_all % 4 == 0 else 1)

    if not has_init:
        @pl.when(c == pl.num_programs(1) - 1)
        def _():
            sfin_ref[0] = s_scr[...]


def _ssd_scan(xc, dt, dtt, prm, s0, *, dims, n_seq, seq_len, row_off):
    h_all, g_all, n_state, p_dim = dims
    inner, gw = h_all * p_dim, (h_all // g_all) * p_dim
    L = CHUNK
    nc = seq_len // L
    off = row_off // L
    cw = xc.shape[1]
    has_init = s0 is not None
    brow, arow, bcol, acol, e_mat = prm

    def fwd(b, c):
        return off + b * nc + c

    def bwd(b, c):
        return off + b * nc + (nc - 1 - c)

    full = lambda a: pl.BlockSpec(a.shape, lambda b, c: (0,) * a.ndim)
    in_specs = [pl.BlockSpec((L, cw), lambda b, c: (fwd(b, c), 0)),
                pl.BlockSpec((L, cw), lambda b, c: (bwd(b, c), 0)),
                pl.BlockSpec((L, 2 * h_all), lambda b, c: (fwd(b, c), 0)),
                pl.BlockSpec((L, 2 * h_all), lambda b, c: (bwd(b, c), 0)),
                pl.BlockSpec((1, 2 * h_all, L), lambda b, c: (fwd(b, c), 0, 0)),
                pl.BlockSpec((1, 2 * h_all, L), lambda b, c: (bwd(b, c), 0, 0)),
                full(brow), full(arow), full(bcol), full(acol), full(e_mat)]
    args = [xc, xc, dt, dt, dtt, dtt, brow, arow, bcol, acol, e_mat]
    st_block = (1, 2, g_all, n_state, gw)
    if has_init:
        in_specs.append(pl.BlockSpec(st_block, lambda b, c: (b, 0, 0, 0, 0)))
        args.append(s0)
    t_loc = n_seq * seq_len
    out_shape = [jax.ShapeDtypeStruct((t_loc, inner), F32)] * 2
    out_specs = [pl.BlockSpec((L, inner), lambda b, c: (b * nc + c, 0)),
                 pl.BlockSpec((L, inner), lambda b, c: (b * nc + (nc - 1 - c), 0))]
    if not has_init:
        out_shape.append(jax.ShapeDtypeStruct((n_seq,) + st_block[1:], F32))
        out_specs.append(pl.BlockSpec(st_block, lambda b, c: (b, 0, 0, 0, 0)))
    scratch = [pltpu.VMEM((2, g_all, n_state, gw), F32),
               pltpu.VMEM((L, inner), F32), pltpu.VMEM((L, inner), F32),
               pltpu.VMEM((L, inner), BF16), pltpu.VMEM((L, inner), BF16),
               pltpu.VMEM((h_all, L), F32), pltpu.VMEM((8, inner), F32)]
    return pl.pallas_call(
        functools.partial(_ssd_kernel, dims=dims, has_init=has_init), grid=(n_seq, nc),
        in_specs=in_specs, out_specs=out_specs, out_shape=out_shape, scratch_shapes=scratch,
        compiler_params=_params("parallel", "arbitrary"), name="ssd_scan")(*args)


def _gla_kernel(*refs, tb, dk, dv, hps, has_init):
    L = HG_CHUNK
    nch = tb // L
    mid = L // 2
    refs = list(refs)
    q_r, v_r, f_r = refs[0:2], refs[2:4], refs[4:6]
    loglb, log1m, onem = refs[6:8], refs[8:10], refs[10:12]
    refs = refs[12:]
    if has_init:
        s0_ref = refs.pop(0)
    o_r = refs[0:2]
    refs = refs[2:]
    if not has_init:
        sfin_ref = refs.pop(0)
    st_scr, = refs

    tblk = pl.program_id(2)

    @pl.when(tblk == 0)
    def _():
        if has_init:
            st_scr[...] = s0_ref[0]
        else:
            st_scr[...] = jnp.zeros(st_scr.shape, F32)

    ri = lax.broadcasted_iota(jnp.int32, (tb, tb), 0)
    ci = lax.broadcasted_iota(jnp.int32, (tb, tb), 1)
    same = (ri // L) == (ci // L)
    rl = lax.broadcasted_iota(jnp.int32, (L, L), 0)
    cl = lax.broadcasted_iota(jnp.int32, (L, L), 1)

    for d in range(2):
        hf = f_r[d][...]
        log_sig = jnp.minimum(hf, 0.0) - jnp.log1p(jnp.exp(-jnp.abs(hf)))
        a = loglb[d][0]
        b = log1m[d][0] + log_sig
        log_f = jnp.maximum(a, b) + jnp.log1p(jnp.exp(-jnp.abs(a - b)))
        k = onem[d][0] * jax.nn.sigmoid(-hf)
        q = _silu(q_r[d][...].astype(F32))
        v = v_r[d][...]
        tri_keep = same & ((ci <= ri) if d == 0 else (ci >= ri))
        cum = _exact_left(tri_keep.astype(BF16), log_f)
        ref_row = mid if d == 0 else mid - 1
        last_row = L - 1 if d == 0 else 0
        refx = jnp.concatenate([jnp.broadcast_to(cum[j * L + ref_row:j * L + ref_row + 1], (L, cum.shape[1]))
                                for j in range(nch)], axis=0)
        lastx = jnp.concatenate([jnp.broadcast_to(cum[j * L + last_row:j * L + last_row + 1], (L, cum.shape[1]))
                                 for j in range(nch)], axis=0)
        qd = (q * jnp.exp(cum - refx)).astype(BF16)
        kd = (k * jnp.exp(refx - cum)).astype(BF16)
        qs = (q * jnp.exp(cum)).astype(BF16)
        ks = (k * jnp.exp(lastx - cum)).astype(BF16)
        elast = jnp.exp(lastx)
        keep = (cl <= rl) if d == 0 else (cl >= rl)
        order = range(nch) if d == 0 else range(nch - 1, -1, -1)
        for hh in range(hps):
            kc = slice(hh * dk, (hh + 1) * dk)
            vc = slice(hh * dv, (hh + 1) * dv)
            rows = [slice(j * L, (j + 1) * L) for j in range(nch)]
            sc = [jnp.where(keep, _dot_nt(qd[sl, kc], kd[sl, kc]), 0.0).astype(BF16) for sl in rows]
            upd = [_dot_tn(v[sl, vc], ks[sl, kc]) for sl in rows]
            intra = [_dot(sc[j], v[rows[j], vc]) for j in range(nch)]
            st = st_scr[d, hh]
            for j in order:
                o_r[d][rows[j], vc] = intra[j] + _dot_nt(qs[rows[j], kc], st.astype(BF16))
                st = st * elast[j * L:j * L + 1, kc] + upd[j]
            st_scr[d, hh] = st

    if not has_init:
        @pl.when(tblk == pl.num_programs(2) - 1)
        def _():
            sfin_ref[0] = st_scr[...]


def _gla_scan(hq, hi, hf, prm, s0, *, heads, dk, dv, n_seq, seq_len, row_off):
    tb = min(256, seq_len)
    ntb = seq_len // tb
    off = row_off // tb
    has_init = s0 is not None
    loglb, log1m, onem = prm
    hps = GLA_HEADS_PER_STEP if heads % GLA_HEADS_PER_STEP == 0 else 1
    hblk = heads // hps

    def fwd(b, h, t):
        return off + b * ntb + t

    def bwd(b, h, t):
        return off + b * ntb + (ntb - 1 - t)

    def tok(rowf, col0):
        return pl.BlockSpec((tb, hps * dk), lambda b, h, t: (rowf(b, h, t), col0 + h))

    def par(d):
        return pl.BlockSpec((1, 1, hps * dk), lambda b, h, t: (d, 0, h))

    in_specs = [tok(fwd, 0), tok(bwd, 0), tok(fwd, 0), tok(bwd, 0), tok(fwd, 0), tok(bwd, hblk),
                par(0), par(1), par(0), par(1), par(0), par(1)]
    args = [hq, hq, hi, hi, hf, hf, loglb, loglb, log1m, log1m, onem, onem]
    st_block = (1, 2, hps, dv, dk)
    st_map = lambda b, h, t: (b, 0, h, 0, 0)
    if has_init:
        in_specs.append(pl.BlockSpec(st_block, st_map))
        args.append(s0)
    t_loc = n_seq * seq_len
    out_shape = [jax.ShapeDtypeStruct((t_loc, heads * dv), F32)] * 2
    out_specs = [pl.BlockSpec((tb, hps * dv), lambda b, h, t: (b * ntb + t, h)),
                 pl.BlockSpec((tb, hps * dv), lambda b, h, t: (b * ntb + (ntb - 1 - t), h))]
    if not has_init:
        out_shape.append(jax.ShapeDtypeStruct((n_seq, 2, heads, dv, dk), F32))
        out_specs.append(pl.BlockSpec(st_block, st_map))
    return pl.pallas_call(
        functools.partial(_gla_kernel, tb=tb, dk=dk, dv=dv, hps=hps, has_init=has_init), grid=(n_seq, hblk, ntb),
        in_specs=in_specs, out_specs=out_specs, out_shape=out_shape,
        scratch_shapes=[pltpu.VMEM((2, hps, dv, dk), F32)],
        compiler_params=_params("parallel", "parallel", "arbitrary"), name="gla_scan")(*args)


def _ret_kernel(*refs, tb, dk, hps, has_init, scale):
    L = CHUNK
    nch = tb // L
    refs = list(refs)
    q_r, k_r, v_r = refs[0:2], refs[2:4], refs[4:6]
    lg_r = refs[6:8]
    refs = refs[8:]
    if has_init:
        cos_r, sin_r = refs[0:2], refs[2:4]
        s0_ref = refs[4]
        refs = refs[5:]
    o_r = refs[0:2]
    refs = refs[2:]
    if not has_init:
        sfin_ref = refs.pop(0)
    s_scr, = refs

    tblk = pl.program_id(2)

    @pl.when(tblk == 0)
    def _():
        if has_init:
            s_scr[...] = s0_ref[0]
        else:
            s_scr[...] = jnp.zeros(s_scr.shape, F32)

    rl = lax.broadcasted_iota(jnp.int32, (L, L), 0)
    cl = lax.broadcasted_iota(jnp.int32, (L, L), 1)
    pos = lax.broadcasted_iota(jnp.int32, (L, dk), 0).astype(F32)

    def rope(x, cos, sin):
        parts = [pltpu.roll(x[:, i:i + LANES], LANES // 2, 1) for i in range(0, dk, LANES)]
        return x * cos + jnp.concatenate(parts, axis=1) * sin

    for d, hh in [(d, hh) for d in range(2) for hh in range(hps)]:
        cols = slice(hh * dk, (hh + 1) * dk)
        lg = lg_r[d][0, hh]
        lgk = jnp.concatenate([lg] * (dk // LANES), axis=1)
        q = q_r[d][:, cols].astype(F32)
        k = k_r[d][:, cols].astype(F32) * scale
        if has_init:
            q = rope(q, cos_r[d][...], sin_r[d][...])
            k = rope(k, cos_r[d][...], sin_r[d][...])
        q = q.astype(BF16)
        k = k.astype(BF16)
        v = v_r[d][:, cols]
        if d == 0:
            keep = cl <= rl
            dmat = jnp.exp(jnp.where(keep, (rl - cl).astype(F32) * lg[:, :L], -jnp.inf))
            w_q = jnp.exp((pos + 1.0) * lgk)
            w_k = jnp.exp((L - 1.0 - pos) * lgk)
        else:
            keep = cl >= rl
            dmat = jnp.exp(jnp.where(keep, (cl - rl).astype(F32) * lg[:, :L], -jnp.inf))
            w_q = jnp.exp((L - pos) * lgk)
            w_k = jnp.exp(pos * lgk)
        e_chunk = jnp.exp(float(L) * lgk)
        s = s_scr[d, hh]
        order = range(nch) if d == 0 else range(nch - 1, -1, -1)
        rows = [slice(j * L, (j + 1) * L) for j in range(nch)]
        sc = [(_dot_nt(q[sl], k[sl]) * dmat).astype(BF16) for sl in rows]
        upd = [_dot_tn((k[sl].astype(F32) * w_k).astype(BF16), v[sl]) for sl in rows]
        intra = [_dot(sc[j], v[rows[j]]) for j in range(nch)]
        for j in order:
            o_r[d][rows[j], cols] = intra[j] + _dot(q[rows[j]], s.astype(BF16)) * w_q
            s = s * e_chunk[:, :1] + upd[j]
        s_scr[d, hh] = s

    if not has_init:
        @pl.when(tblk == pl.num_programs(2) - 1)
        def _():
            sfin_ref[0] = s_scr[...]


def _ret_scan(rq, rk, rv, lg, rope, s0, *, heads, dk, dv, n_seq, seq_len, row_off):
    assert dk == dv
    tb = min(256, seq_len)
    ntb = seq_len // tb
    off = row_off // tb
    has_init = s0 is not None

    def fwd(b, h, t):
        return off + b * ntb + t

    def bwd(b, h, t):
        return off + b * ntb + (ntb - 1 - t)

    hps = RET_HEADS_PER_STEP if heads % RET_HEADS_PER_STEP == 0 else 1

    def tok(rowf):
        return pl.BlockSpec((tb, hps * dk), lambda b, h, t: (rowf(b, h, t), h))

    def lgs(d):
        return pl.BlockSpec((1, hps, 1, LANES), lambda b, h, t: (d, h, 0, 0))

    in_specs = [tok(fwd), tok(bwd), tok(fwd), tok(bwd), tok(fwd), tok(bwd), lgs(0), lgs(1)]
    args = [rq, rq, rk, rk, rv, rv, lg, lg]
    st_block = (1, 2, hps, dk, dv)
    st_map = lambda b, h, t: (b, 0, h, 0, 0)
    if has_init:
        cos, sin = rope
        pf = pl.BlockSpec((tb, dk), lambda b, h, t: (t, 0))
        pb = pl.BlockSpec((tb, dk), lambda b, h, t: (ntb - 1 - t, 0))
        in_specs += [pf, pb, pf, pb, pl.BlockSpec(st_block, st_map)]
        args += [cos, cos, sin, sin, s0]
    t_loc = n_seq * seq_len
    out_shape = [jax.ShapeDtypeStruct((t_loc, heads * dv), F32)] * 2
    out_specs = [pl.BlockSpec((tb, hps * dv), lambda b, h, t: (b * ntb + t, h)),
                 pl.BlockSpec((tb, hps * dv), lambda b, h, t: (b * ntb + (ntb - 1 - t), h))]
    if not has_init:
        out_shape.append(jax.ShapeDtypeStruct((n_seq, 2, heads, dk, dv), F32))
        out_specs.append(pl.BlockSpec(st_block, st_map))
    return pl.pallas_call(
        functools.partial(_ret_kernel, tb=tb, dk=dk, hps=hps, has_init=has_init, scale=dk ** -0.5),
        grid=(n_seq, heads // hps, ntb),
        in_specs=in_specs, out_specs=out_specs, out_shape=out_shape,
        scratch_shapes=[pltpu.VMEM((2, hps, dk, dv), F32)],
        compiler_params=_params("parallel", "parallel", "arbitrary"), name="ret_scan")(*args)


def _grms_kernel(*refs, gs, gate_before, has_skip, ctx_blocks):
    refs = list(refs)
    ac_ref, bc_ref, al_ref, bl_ref, gate_ref, w_ref = refs[:6]
    refs = refs[6:]
    if has_skip:
        x_ref, d_ref = refs[:2]
        refs = refs[2:]
    o_ref, = refs
    y = jnp.where(pl.program_id(0) < ctx_blocks, ac_ref[...] + bc_ref[...], al_ref[...] + bl_ref[...])
    if has_skip:
        y = y + d_ref[...] * x_ref[...].astype(F32)
    gate = _silu(gate_ref[...].astype(F32))
    if gate_before:
        y = y * gate
    width = y.shape[1]
    outs = []
    for g in range(width // gs):
        yg = y[:, g * gs:(g + 1) * gs]
        ms = jnp.mean(yg * yg, axis=-1, keepdims=True)
        outs.append(yg * lax.rsqrt(ms + EPS))
    yn = jnp.concatenate(outs, axis=1) * w_ref[...]
    if not gate_before:
        yn = yn * gate
    o_ref[...] = yn.astype(o_ref.dtype)


def _group_rms_gate(geom, ab_ctx, ab_lat, gate, gate_col0, w, *, gs, gate_before, skip=None):
    t = geom.t_all
    width = ab_ctx[0].shape[1]
    rb = geom.rb
    cb = max(_tile(width, 2048), gs)
    assert width % cb == 0 and cb % gs == 0 and gate_col0 % cb == 0
    nctx, ncb = geom.ctx_blocks, width // cb
    row = pl.BlockSpec((rb, cb), lambda i, j: (i, j))
    vec = pl.BlockSpec((1, cb), lambda i, j: (0, j))
    ctx = pl.BlockSpec((rb, cb), lambda i, j: (jnp.minimum(i, nctx - 1), jnp.where(i < nctx, j, ncb - 1)))
    lat = pl.BlockSpec((rb, cb), lambda i, j: (jnp.maximum(i - nctx, 0), jnp.where(i < nctx, 0, j)))
    g0 = gate_col0 // cb
    in_specs = [ctx, ctx, lat, lat, pl.BlockSpec((rb, cb), lambda i, j: (i, g0 + j)), vec]
    args = [*ab_ctx, *ab_lat, gate, w.reshape(1, width)]
    if skip is not None:
        x, dvec = skip
        in_specs += [row, vec]
        args += [x, dvec.reshape(1, width)]
    return pl.pallas_call(
        functools.partial(_grms_kernel, gs=gs, gate_before=gate_before, has_skip=skip is not None, ctx_blocks=nctx),
        grid=(t // rb, width // cb), in_specs=in_specs, out_specs=row,
        out_shape=jax.ShapeDtypeStruct((t, width), BF16),
        compiler_params=_params("parallel", "parallel"), name="group_rms_gate")(*args)


def _merge_kernel(gs_ref, gh_ref, gr_ref, ys_ref, yh_ref, yr_ref, ws_ref, wh_ref, wr_ref, o_ref):
    acc = jax.nn.sigmoid(gs_ref[...].astype(F32)) * _dot(ys_ref[...], ws_ref[...])
    acc = acc + jax.nn.sigmoid(gh_ref[...].astype(F32)) * _dot(yh_ref[...], wh_ref[...])
    acc = acc + jax.nn.sigmoid(gr_ref[...].astype(F32)) * _dot(yr_ref[...], wr_ref[...])
    o_ref[...] = acc.astype(o_ref.dtype)


def _branch_merge(bgate, ys, yh, yr, ws, wh, wr, tm=512, tn=512):
    t = ys.shape[0]
    d = ws.shape[1]
    tm, tn = _tile(t, tm), _tile(d, tn)
    nb = d // tn
    gate = lambda k: pl.BlockSpec((tm, tn), lambda i, j: (i, k * nb + j))
    act = lambda y: pl.BlockSpec((tm, y.shape[1]), lambda i, j: (i, 0), pipeline_mode=pl.Buffered(1))
    wgt = lambda w: pl.BlockSpec((w.shape[0], tn), lambda i, j: (0, j))
    return pl.pallas_call(
        _merge_kernel, grid=(t // tm, nb),
        in_specs=[gate(0), gate(1), gate(2), act(ys), act(yh), act(yr), wgt(ws), wgt(wh), wgt(wr)],
        out_specs=pl.BlockSpec((tm, tn), lambda i, j: (i, j)), out_shape=jax.ShapeDtypeStruct((t, d), BF16),
        compiler_params=_params("parallel", "arbitrary"), name="branch_merge")(
            bgate, bgate, bgate, ys, yh, yr, ws, wh, wr)


def _topk_desc(s, k):
    rows = []
    cur = s
    for _ in range(k):
        m = jnp.max(cur, axis=0, keepdims=True)
        rows.append(m)
        cur = jnp.where(cur == m, -jnp.inf, cur)
    return rows


def _route_kernel(q_ref, keys_ref, a1_ref, s1_ref, a2_ref, s2_ref, tau_ref, *, heads, nk, half, topk):
    tm = q_ref.shape[0]
    row_k = lax.broadcasted_iota(jnp.int32, (topk, tm), 0)
    row_t = lax.broadcasted_iota(jnp.int32, (tau_ref.shape[0], tm), 0)
    tau_all = jnp.zeros(tau_ref.shape, F32)
    for h in range(heads):
        tops, masked = [], []
        for c in range(2):
            qh = q_ref[:, (2 * h + c) * half:(2 * h + c + 1) * half].astype(BF16)
            s = _dot_nt(keys_ref[h, c], qh)
            rows = _topk_desc(s, topk)
            tops.append(rows)
            masked.append(jnp.where(s >= rows[-1], s, MASKED))
        v2 = jnp.zeros((topk, tm), F32)
        for b in range(topk):
            v2 = jnp.where(row_k == b, tops[1][b], v2)
        cand = jnp.concatenate([tops[0][a] + v2 for a in range(topk)], axis=0)
        best = _topk_desc(cand, topk)
        tau = best[-1]
        z = sum(jnp.exp(b - best[0]) for b in best)
        inv_z = 1.0 / z
        a1_ref[h * nk:(h + 1) * nk, :] = jnp.exp(masked[0] - tops[0][0]) * inv_z
        a2_ref[h * nk:(h + 1) * nk, :] = jnp.exp(masked[1] - tops[1][0])
        s1_ref[h * nk:(h + 1) * nk, :] = masked[0]
        s2_ref[h * nk:(h + 1) * nk, :] = masked[1]
        tau_all = jnp.where(row_t == h, tau, tau_all)
    tau_ref[...] = tau_all


def _route(q, keys, *, topk):
    t = q.shape[0]
    heads, _, nk, half = keys.shape
    tm = _tile(t, 256)
    assert tm % LANES == 0
    tau_rows = 8 * ((heads + 7) // 8)
    col = lambda r: pl.BlockSpec((r, tm), lambda i: (0, i))
    wide = jax.ShapeDtypeStruct((heads * nk, t), F32)
    return pl.pallas_call(
        functools.partial(_route_kernel, heads=heads, nk=nk, half=half, topk=topk), grid=(t // tm,),
        in_specs=[pl.BlockSpec((tm, q.shape[1]), lambda i: (i, 0)), pl.BlockSpec(keys.shape, lambda i: (0, 0, 0, 0))],
        out_specs=[col(heads * nk)] * 4 + [col(tau_rows)],
        out_shape=[wide] * 4 + [jax.ShapeDtypeStruct((tau_rows, t), F32)],
        compiler_params=_params("parallel"), name="peer_route")(q, keys)


def _peer_kernel(x_ref, u_ref, v_ref, a1_ref, s1_ref, a2_ref, s2_ref, tau_ref, o_ref, w_even, w_odd, *,
                 heads, nk, n_tiles):
    j = pl.program_id(1)
    tn = u_ref.shape[0]
    nsub = tn // nk
    tile = jnp.minimum(j, n_tiles - 1)

    @pl.when(j == 0)
    def _():
        w_even[...] = jnp.zeros(w_even.shape, BF16)
        o_ref[...] = jnp.zeros(o_ref.shape, F32)

    def step(w_prev, w_next):
        o_ref[...] += _dot(w_prev[...], v_ref[...])
        x = x_ref[...]
        gates = []
        for e in range(nsub):
            e1 = tile * nsub + e
            gate_t = jnp.zeros((nk, x.shape[0]), F32)
            for h in range(heads):
                s1 = s1_ref[pl.ds(h * nk + e1, 1), :]
                a1 = a1_ref[pl.ds(h * nk + e1, 1), :]
                hit = (s1 + s2_ref[h * nk:(h + 1) * nk, :]) >= tau_ref[h:h + 1, :]
                gate_t = gate_t + jnp.where(hit, a1 * a2_ref[h * nk:(h + 1) * nk, :], 0.0)
            gates.append(gate_t.T)
        wide = min(tn, MXU_WIDTH)
        for p in range(tn // wide):
            pre = _dot_nt(x, u_ref[p * wide:(p + 1) * wide, :])
            act = 0.5 * pre * (1.0 + lax.erf(pre * (1.0 / math.sqrt(2.0))))
            for q in range(wide // nk):
                blk = act[:, q * nk:(q + 1) * nk] * gates[p * (wide // nk) + q]
                w_next[:, p * wide + q * nk:p * wide + (q + 1) * nk] = blk.astype(BF16)

    @pl.when(j % 2 == 0)
    def _():
        step(w_even, w_odd)

    @pl.when(j % 2 == 1)
    def _():
        step(w_odd, w_even)


def _peer(x, u, v, route, *, heads, nk, tm=512, tn=512):
    t, d = x.shape
    e_all = u.shape[0]
    tm, tn = _tile(t, tm), _tile(e_all, tn)
    assert tn % nk == 0 and tm % LANES == 0
    a1, s1, a2, s2, tau = route
    once = dict(pipeline_mode=pl.Buffered(1))
    nt = e_all // tn
    rowx = pl.BlockSpec((tm, d), lambda i, j: (i, 0), **once)
    tab_u = pl.BlockSpec((tn, d), lambda i, j: (jnp.minimum(j, nt - 1), 0))
    tab_v = pl.BlockSpec((tn, d), lambda i, j: (jnp.maximum(j - 1, 0), 0))
    wide = pl.BlockSpec((heads * nk, tm), lambda i, j: (0, i), **once)
    taus = pl.BlockSpec((tau.shape[0], tm), lambda i, j: (0, i), **once)
    return pl.pallas_call(
        functools.partial(_peer_kernel, heads=heads, nk=nk, n_tiles=nt), grid=(t // tm, nt + 1),
        in_specs=[rowx, tab_u, tab_v, wide, wide, wide, wide, taus],
        out_specs=pl.BlockSpec((tm, d), lambda i, j: (i, 0)), out_shape=jax.ShapeDtypeStruct((t, d), F32),
        scratch_shapes=[pltpu.VMEM((tm, tn), BF16), pltpu.VMEM((tm, tn), BF16)],
        compiler_params=_params("parallel", "arbitrary"), name="peer_dense")(x, u, v, a1, s1, a2, s2, tau)


def _rope_tables(seq_len, dk):
    half, quarter = dk // 2, dk // 4
    freqs = ROPE_BASE ** (-jnp.arange(quarter, dtype=F32) / quarter)
    t = jnp.arange(seq_len)
    ang_r = (t // GRID_W).astype(F32)[:, None] * freqs
    ang_c = (t % GRID_W).astype(F32)[:, None] * freqs
    cos = jnp.concatenate([jnp.cos(ang_r)] * 2 + [jnp.cos(ang_c)] * 2, axis=1)
    sin = jnp.concatenate([-jnp.sin(ang_r), jnp.sin(ang_r), -jnp.sin(ang_c), jnp.sin(ang_c)], axis=1)
    return cos, sin


def kernel(x_prompt, x_sample, c, state_ssm, state_hgrn, state_ret, c_ctx, w_ada, b_ada, w_in, m_conv_w, m_conv_b, m_dt_bias, m_a_log, m_d, m_norm, hg_lower_bounds, hg_norm, ret_decay, ret_norm, w_br_ssm, w_br_hg, w_br_ret, w_out, ln1_g, ln1_b, ln2_g, ln2_b, pk_query, pk_keys, peer_u, peer_v):
    bp, seq, d_model = x_prompt.shape
    bl, dec_seq, _ = x_sample.shape
    depth = w_in.shape[0]
    m_heads, m_state, m_headdim = state_ssm.shape[3:]
    hg_heads, hg_dk, hg_dv = state_hgrn.shape[3:]
    ret_heads, ret_dk, ret_dv = state_ret.shape[3:]
    m_inner = m_heads * m_headdim
    m_bc = M_GROUPS * m_state
    hg_w = hg_heads * hg_dk
    ret_w = ret_heads * ret_dk
    pk_heads, _, n_keys, pk_half = pk_keys.shape[1:]
    hpg = m_heads // M_GROUPS
    gw = hpg * m_headdim
    alpha = (2 * depth) ** 0.25
    geom = Geom(bp, seq, bl, dec_seq)
    t_ctx = geom.t_ctx
    ssd_dims = (m_heads, M_GROUPS, m_state, m_headdim)

    sizes = (m_inner, m_inner + 2 * m_bc, 2 * m_heads, hg_w, 2 * hg_w, hg_w, hg_w, ret_w, ret_w, ret_w, ret_w, 3 * d_model)
    offs = [0]
    for s in sizes:
        offs.append(offs[-1] + s)
    seg_dtype = (BF16, BF16, F32, BF16, F32, BF16, BF16, BF16, BF16, BF16, BF16, BF16)

    x = jnp.concatenate([x_prompt.reshape(t_ctx, d_model), x_sample.reshape(geom.t_lat, d_model)], axis=0)
    rows = 8 * ((1 + bl + 7) // 8)
    cvec = jnp.zeros((rows, d_model), F32).at[0].set(c_ctx).at[1:1 + bl].set(c)

    p_lb = jax.nn.softmax(hg_lower_bounds.astype(F32), axis=0)
    lower = jnp.cumsum(p_lb, axis=0) - p_lb[:1]
    rope = _rope_tables(dec_seq, ret_dk)
    e_mat = jnp.repeat(jnp.eye(m_heads, dtype=BF16), m_headdim, axis=1)
    log_g = jax.nn.log_sigmoid(ret_decay.astype(F32))

    out_ssm, out_hg, out_ret = [], [], []
    h = None
    modtabs = [_ada(cvec, w_ada[l], b_ada[l]).reshape(rows * N_MOD, 1, d_model) for l in range(depth)]
    h, = _ln_mod(geom, x, alpha, mod=(modtabs[0], 1, 0))
    for l in range(depth):
        modtab = modtabs[l]

        segs = [_matmul(h, w_in[l][:, offs[k]:offs[k + 1]].astype(BF16), seg_dtype[k])
                for k in range(len(sizes))]
        z, xbc, dt_raw, hq, hf, hi, hgate, rq, rk, rv, rgate, bgate = segs

        xc = _conv_silu(geom, xbc, m_conv_w[l], m_conv_b[l])
        dtt = dt_raw.reshape(geom.t_all // CHUNK, CHUNK, 2 * m_heads).transpose(0, 2, 1)
        bias = m_dt_bias[l].reshape(1, 2 * m_heads)
        neg_a = -jnp.exp(m_a_log[l].astype(F32)).reshape(1, 2 * m_heads)
        prm = (bias, neg_a, bias.T, neg_a.T, e_mat)
        s0 = state_ssm[:, l].reshape(bl, 2, M_GROUPS, hpg, m_state, m_headdim)
        s0 = s0.transpose(0, 1, 2, 4, 3, 5).reshape(bl, 2, M_GROUPS, m_state, gw)
        yf_c, yb_c, sfin = _ssd_scan(xc, dt_raw, dtt, prm, None, dims=ssd_dims, n_seq=bp, seq_len=seq, row_off=0)
        yf_l, yb_l = _ssd_scan(xc, dt_raw, dtt, prm, s0, dims=ssd_dims, n_seq=bl, seq_len=dec_seq, row_off=t_ctx)
        sfin = sfin.reshape(bp, 2, M_GROUPS, m_state, hpg, m_headdim).transpose(0, 1, 2, 4, 3, 5)
        out_ssm.append(sfin.reshape(bp, 2, m_heads, m_state, m_headdim))
        y_ssm = _group_rms_gate(geom, (yf_c, yb_c), (yf_l, yb_l),
                                z, 0, m_norm[l], gs=m_inner // M_GROUPS, gate_before=True,
                                skip=(xc, jnp.repeat(m_d[l], m_headdim)))

        lb = lower[l]
        prm = tuple(a.reshape(2, 1, hg_w) for a in (jnp.log(lb), jnp.log1p(-lb), 1.0 - lb))
        s0 = jnp.swapaxes(state_hgrn[:, l], -1, -2)
        of_c, ob_c, gfin = _gla_scan(hq, hi, hf, prm, None, heads=hg_heads, dk=hg_dk, dv=hg_dv,
                                     n_seq=bp, seq_len=seq, row_off=0)
        of_l, ob_l = _gla_scan(hq, hi, hf, prm, s0, heads=hg_heads, dk=hg_dk, dv=hg_dv,
                               n_seq=bl, seq_len=dec_seq, row_off=t_ctx)
        out_hg.append(jnp.swapaxes(gfin, -1, -2))
        y_hg = _group_rms_gate(geom, (of_c, ob_c), (of_l, ob_l),
                               hgate, 0, hg_norm[l], gs=hg_dk, gate_before=False)

        lg = jnp.broadcast_to(log_g[l][:, :, None, None], (2, ret_heads, 1, LANES))
        rf_c, rb_c, tfin = _ret_scan(rq, rk, rv, lg, None, None, heads=ret_heads, dk=ret_dk, dv=ret_dv,
                                     n_seq=bp, seq_len=seq, row_off=0)
        rf_l, rb_l = _ret_scan(rq, rk, rv, lg, rope, state_ret[:, l], heads=ret_heads, dk=ret_dk, dv=ret_dv,
                               n_seq=bl, seq_len=dec_seq, row_off=t_ctx)
        out_ret.append(tfin)
        y_ret = _group_rms_gate(geom, (rf_c, rb_c), (rf_l, rb_l),
                                rgate, 0, ret_norm[l], gs=ret_dk, gate_before=False)

        merged = _branch_merge(bgate, y_ssm, y_hg, y_ret, w_br_ssm[l].astype(BF16), w_br_hg[l].astype(BF16),
                               w_br_ret[l].astype(BF16))
        mix = _matmul(merged, w_out[l].astype(BF16), F32)
        x, h = _ln_mod(geom, x, alpha, ln=(mix, modtab, 2, ln1_g[l], ln1_b[l]), mod=(modtab, 4, 3))

        q = _matmul(h, pk_query[l].astype(BF16), F32)
        route = _route(q, pk_keys[l].astype(BF16), topk=PK_TOPK)
        ff = _peer(h, peer_u[l].astype(BF16), peer_v[l].astype(BF16), route, heads=pk_heads, nk=n_keys)
        if l + 1 < depth:
            x, h = _ln_mod(geom, x, alpha, ln=(ff, modtab, 5, ln2_g[l], ln2_b[l]), mod=(modtabs[l + 1], 1, 0))
        else:
            x, = _ln_mod(geom, x, alpha, ln=(ff, modtab, 5, ln2_g[l], ln2_b[l]))

    dt_out = x_prompt.dtype
    y_p = x[:t_ctx].reshape(bp, seq, d_model)
    y_s = x[t_ctx:].reshape(bl, dec_seq, d_model)
    return (y_p, y_s, jnp.stack(out_ssm, axis=1).astype(dt_out), jnp.stack(out_hg, axis=1).astype(dt_out),
            jnp.stack(out_ret, axis=1).astype(dt_out))
```
